```python
import jax
import jax.numpy as jnp
from jax import lax
import numpy as np

D_MODEL = 4096
BATCH = 4
SEQ = 2048
DEPTH = 1

GRID_W = 64
CTX_LEN = 256
NORM_EPS = 1e-6
N_MOD = 6

M_HEADS = 8
M_DQK = 256
M_DV = 512
M_CHUNK = 64
FORGET_BIAS = 3.0

A_HEADS = 32
A_NOPE = 128
A_ROPE = 64
A_DV = 128
Q_LORA = 1024
KV_LORA = 512
Q_BLOCK = 128
ROPE_BASE = 10000.0
A_SCALE = (A_NOPE + A_ROPE) ** -0.5

N_EXPERTS = 32
TOP_K = 4
D_EXPERT = 1536
SWIGLU_LIMIT = 7.0
SWIGLU_ALPHA = 1.702
EXPERT_BLOCK = 128

M_QK_W = M_HEADS * M_DQK
M_V_W = M_HEADS * M_DV
A_V_W = A_HEADS * A_DV
N_GATE_COLS = 4 * M_HEADS
IN_SPLITS = (M_QK_W, M_QK_W, M_V_W, M_V_W, N_GATE_COLS, Q_LORA, KV_LORA, A_ROPE, D_MODEL, D_MODEL)
D_IN = sum(IN_SPLITS)
IN_SPLIT_POINTS = tuple(int(s) for s in np.cumsum(IN_SPLITS)[:-1])

kernel_name = "hybrid_mlstm_mla_moe_diffusion_block"


def rmsnorm(x, g):
    xf = x.astype(jnp.float32)
    y = xf * lax.rsqrt(jnp.mean(xf * xf, axis=-1, keepdims=True) + NORM_EPS)
    return y.astype(x.dtype) * g


def axial_rope_tables(rows, dtype):
    r, col = jnp.meshgrid(jnp.arange(rows, dtype=jnp.float32), jnp.arange(GRID_W, dtype=jnp.float32), indexing="ij")
    n_freq = A_ROPE // 4
    inv = ROPE_BASE ** (-jnp.arange(n_freq, dtype=jnp.float32) / n_freq)
    ang = jnp.concatenate([r.reshape(-1, 1) * inv, col.reshape(-1, 1) * inv], axis=-1)
    return jnp.cos(ang).astype(dtype), jnp.sin(ang).astype(dtype)


def apply_rope(x, cos, sin):
    xp = x.reshape(*x.shape[:-1], A_ROPE // 2, 2)
    x1, x2 = xp[..., 0], xp[..., 1]
    bshape = (1, cos.shape[0]) + (1,) * (x.ndim - 3) + (cos.shape[1],)
    c, s = cos.reshape(bshape), sin.reshape(bshape)
    return jnp.stack([x1 * c - x2 * s, x1 * s + x2 * c], axis=-1).reshape(x.shape)


def mlstm_prep(zq, zk, zv, zg, b_gates):
    B, S, _ = zq.shape
    q = zq.reshape(B, S, M_HEADS, M_DQK)
    k = zk.reshape(B, S, M_HEADS, M_DQK) * (M_DQK ** -0.5)
    v = zv.reshape(B, S, M_HEADS, M_DV)
    g = zg.astype(jnp.float32) + b_gates.astype(jnp.float32)
    i_f, f_f, i_b, f_b = jnp.split(g, 4, axis=-1)
    return q, k, v, (i_f, jax.nn.log_sigmoid(f_f)), (i_b, jax.nn.log_sigmoid(f_b))


def mlstm_zero_state(B):
    return (jnp.zeros((B, M_HEADS, M_DV, M_DQK), jnp.float32),
            jnp.zeros((B, M_HEADS, M_DQK), jnp.float32),
            jnp.zeros((B, M_HEADS), jnp.float32))


def mlstm_scan(q, k, v, i_pre, log_f, state):
    B, S = q.shape[:2]
    nc = S // M_CHUNK

    def to_chunks(a):
        return a.reshape(B, nc, M_CHUNK, *a.shape[2:]).swapaxes(0, 1)

    xs = tuple(to_chunks(a) for a in (q, k, v, i_pre, log_f))
    scan_order = jnp.tril(jnp.ones((M_CHUNK, M_CHUNK), dtype=bool))

    def step(carry, xc):
        C, n, m = carry
        qc, kc, vc, ic, fc = xc
        bT = jnp.cumsum(fc, axis=1).transpose(0, 2, 1)
        icT = ic.transpose(0, 2, 1)
        d = bT[:, :, :, None] - bT[:, :, None, :] + icT[:, :, None, :]
        d = jnp.where(scan_order, d, -jnp.inf)
        inter = bT + m[:, :, None]
        m_t = jnp.maximum(inter, jnp.max(d, axis=-1))
        w = jnp.exp(d - m_t[..., None])
        scale = jnp.exp(inter - m_t)
        s = jnp.einsum("blhk,bshk->bhls", qc, kc) * w
        num = (jnp.einsum("bhls,bshv->blhv", s, vc)
               + scale.transpose(0, 2, 1)[..., None] * jnp.einsum("bhvk,blhk->blhv", C, qc))
        den = jnp.sum(s, axis=-1) + scale * jnp.einsum("bhk,blhk->bhl", n, qc)
        h = num / jnp.maximum(jnp.abs(den), jnp.exp(-m_t)).transpose(0, 2, 1)[..., None]
        b_last = bT[:, :, -1]
        g = b_last[:, :, None] - bT + icT
        m_new = jnp.maximum(b_last + m, jnp.max(g, axis=-1))
        wk = jnp.exp(g - m_new[..., None])
        dec = jnp.exp(b_last + m - m_new)
        C_new = dec[..., None, None] * C + jnp.einsum("bhs,bshv,bshk->bhvk", wk, vc, kc)
        n_new = dec[..., None] * n + jnp.einsum("bhs,bshk->bhk", wk, kc)
        return (C_new, n_new, m_new), h

    state, hs = lax.scan(step, state, xs)
    return hs.swapaxes(0, 1).reshape(B, S, M_HEADS, M_DV), state


def mlstm_scan_reversed(q, k, v, i_pre, log_f, state):
    rev = lambda a: jnp.flip(a, axis=1)
    h, st = mlstm_scan(rev(q), rev(k), rev(v), rev(i_pre), rev(log_f), state)
    return rev(h), st


def mlstm_output(h_sum, zo, m_out_norm):
    B, S = h_sum.shape[:2]
    hn = h_sum * lax.rsqrt(jnp.mean(h_sum * h_sum, axis=-1, keepdims=True) + NORM_EPS)
    hn = hn.reshape(B, S, M_V_W).astype(zo.dtype) * m_out_norm
    return hn * jax.nn.sigmoid(zo)


def mla_q(zcq, q_norm, w_uq):
    B, S, _ = zcq.shape
    q = (rmsnorm(zcq, q_norm) @ w_uq).reshape(B, S, A_HEADS, A_NOPE + A_ROPE)
    return q[..., :A_NOPE], q[..., A_NOPE:]


def mla_kv(zckv, zkr, kv_norm, w_ukv):
    B, S, _ = zckv.shape
    kv = (rmsnorm(zckv, kv_norm) @ w_ukv).reshape(B, S, A_HEADS, A_NOPE + A_DV)
    return kv[..., :A_NOPE], zkr, kv[..., A_NOPE:]


def dense_attention(qn, qr, kn, kr, v):
    s = jnp.einsum("bqhd,bkhd->bhqk", qn, kn) + jnp.einsum("bqhr,bkr->bhqk", qr, kr)
    p = jax.nn.softmax(s.astype(jnp.float32) * A_SCALE, axis=-1).astype(v.dtype)
    return jnp.einsum("bhqk,bkhv->bqhv", p, v)


def latent_attention(qn, qr, kn, kr, v, ckn, ckr, cv):
    B, S = qn.shape[:2]
    kn_all = jnp.concatenate([ckn, kn], axis=1)
    kr_all = jnp.concatenate([ckr, kr], axis=1)
    v_all = jnp.concatenate([cv, v], axis=1)
    nb = S // Q_BLOCK

    def blocks(a):
        return a.reshape(B, nb, Q_BLOCK, *a.shape[2:]).swapaxes(0, 1)

    o = lax.map(lambda qs: dense_attention(qs[0], qs[1], kn_all, kr_all, v_all), (blocks(qn), blocks(qr)))
    return o.swapaxes(0, 1).reshape(B, S, A_V_W)


def merge_branches(ya, yb, zga, zgb, w_branch_a, w_branch_b, w_out):
    return (jax.nn.sigmoid(zga) * (ya @ w_branch_a) + jax.nn.sigmoid(zgb) * (yb @ w_branch_b)) @ w_out


def token_mixer(h, hc, cos, sin, w_in, b_gates, m_out_norm, q_norm, kv_norm, w_uq, w_ukv,
                w_branch_a, w_branch_b, w_out, with_ctx_out):
    B = h.shape[0]
    zq, zk, zv, zo, zg, zcq, zckv, zkr, zga, zgb = jnp.split(h @ w_in, IN_SPLIT_POINTS, axis=-1)
    czq, czk, czv, czo, czg, czcq, czckv, czkr, czga, czgb = jnp.split(hc @ w_in, IN_SPLIT_POINTS, axis=-1)

    q, k, v, gates_f, gates_b = mlstm_prep(zq, zk, zv, zg, b_gates)
    cq, ck, cv, cgates_f, cgates_b = mlstm_prep(czq, czk, czv, czg, b_gates)
    st0 = mlstm_zero_state(B)
    hc_f, st_f = mlstm_scan(cq, ck, cv, cgates_f[0], cgates_f[1], st0)
    hc_b, st_b = mlstm_scan_reversed(cq, ck, cv, cgates_b[0], cgates_b[1], st0)
    h_f, _ = mlstm_scan(q, k, v, gates_f[0], gates_f[1], st_f)
    h_b, _ = mlstm_scan_reversed(q, k, v, gates_b[0], gates_b[1], st_b)
    ya = mlstm_output(h_f + h_b, zo, m_out_norm)

    kn, kr, va = mla_kv(zckv, zkr, kv_norm, w_ukv)
    ckn, ckr, cva = mla_kv(czckv, czkr, kv_norm, w_ukv)
    qn, qr = mla_q(zcq, q_norm, w_uq)
    yb = latent_attention(qn, apply_rope(qr, cos, sin), kn, apply_rope(kr, cos, sin), va, ckn, ckr, cva)

    y = merge_branches(ya, yb, zga, zgb, w_branch_a, w_branch_b, w_out)
    if not with_ctx_out:
        return y, None
    ya_c = mlstm_output(hc_f + hc_b, czo, m_out_norm)
    cqn, cqr = mla_q(czcq, q_norm, w_uq)
    yb_c = dense_attention(cqn, cqr, ckn, ckr, cva).reshape(hc.shape[0], hc.shape[1], A_V_W)
    yc = merge_branches(ya_c, yb_c, czga, czgb, w_branch_a, w_branch_b, w_out)
    return y, yc


def moe_ffn(h, router_w, router_b, w_gu, b_gu, w_down, b_down):
    T = h.shape[0]
    logits = (h @ router_w + router_b).astype(jnp.float32)
    top_logit, top_idx = lax.top_k(logits, TOP_K)
    gates = jax.nn.softmax(top_logit, axis=-1)
    n_assign = T * TOP_K
    n_blocks = -(-n_assign // EXPERT_BLOCK) + N_EXPERTS
    n_slots = n_blocks * EXPERT_BLOCK
    flat_e = top_idx.reshape(-1)
    flat_tok = jnp.broadcast_to(jnp.arange(T, dtype=jnp.int32)[:, None], (T, TOP_K)).reshape(-1)
    flat_g = gates.reshape(-1)
    order = jnp.argsort(flat_e, stable=True)
    sorted_e = flat_e[order]
    counts = jnp.bincount(flat_e, length=N_EXPERTS)
    padded = (counts + EXPERT_BLOCK - 1) // EXPERT_BLOCK * EXPERT_BLOCK
    pad_end = jnp.cumsum(padded)
    pad_start = pad_end - padded
    start = jnp.cumsum(counts) - counts
    dest = pad_start[sorted_e] + jnp.arange(n_assign, dtype=jnp.int32) - start[sorted_e]
    slot_tok = jnp.zeros((n_slots,), jnp.int32).at[dest].set(flat_tok[order])
    slot_gate = jnp.zeros((n_slots,), jnp.float32).at[dest].set(flat_g[order])
    block_e = jnp.minimum(jnp.searchsorted(pad_end, jnp.arange(n_blocks, dtype=jnp.int32) * EXPERT_BLOCK, side="right"), N_EXPERTS - 1)

    def expert_block(args):
        toks, e = args
        gu = h[toks] @ w_gu[e] + b_gu[e]
        glu = jnp.minimum(gu[:, :D_EXPERT], SWIGLU_LIMIT)
        lin = jnp.clip(gu[:, D_EXPERT:], -SWIGLU_LIMIT, SWIGLU_LIMIT)
        act = glu * jax.nn.sigmoid(SWIGLU_ALPHA * glu) * (lin + 1)
        return act @ w_down[e] + b_down[e]

    ys = lax.map(expert_block, (slot_tok.reshape(n_blocks, EXPERT_BLOCK), block_e))
    ys = ys.reshape(n_slots, -1) * slot_gate[:, None].astype(h.dtype)
    return jnp.zeros_like(h).at[slot_tok].add(ys)


def trunk_layer(x, xc, c_silu, cc_silu, cos, sin, w_ada, b_ada, norm_pre_mix, norm_post_mix,
                norm_pre_ffn, norm_post_ffn, w_in, b_gates, m_out_norm, q_norm, kv_norm, w_uq, w_ukv,
                w_branch_a, w_branch_b, w_out, router_w, router_b, w_gu, b_gu, w_down, b_down, with_ctx_out):
    B, S, D = x.shape
    sh1, sc1, g1, sh2, sc2, g2 = jnp.split((c_silu @ w_ada + b_ada)[:, None, :], N_MOD, axis=-1)
    csh1, csc1, cg1, csh2, csc2, cg2 = jnp.split((cc_silu @ w_ada + b_ada)[None, None, :], N_MOD, axis=-1)
    h = rmsnorm(x, norm_pre_mix) * (1 + sc1) + sh1
    hc = rmsnorm(xc, norm_pre_mix) * (1 + csc1) + csh1
    y, yc = token_mixer(h, hc, cos, sin, w_in, b_gates, m_out_norm, q_norm, kv_norm, w_uq, w_ukv,
                        w_branch_a, w_branch_b, w_out, with_ctx_out)
    x = x + g1 * rmsnorm(y, norm_post_mix)
    h2 = rmsnorm(x, norm_pre_ffn) * (1 + sc2) + sh2
    f = moe_ffn(h2.reshape(B * S, D), router_w, router_b, w_gu, b_gu, w_down, b_down).reshape(B, S, D)
    x = x + g2 * rmsnorm(f, norm_post_ffn)
    if with_ctx_out:
        xc = xc + cg1 * rmsnorm(yc, norm_post_mix)
        h2c = rmsnorm(xc, norm_pre_ffn) * (1 + csc2) + csh2
        fc = moe_ffn(h2c.reshape(-1, D), router_w, router_b, w_gu, b_gu, w_down, b_down).reshape(xc.shape)
        xc = xc + cg2 * rmsnorm(fc, norm_post_ffn)
    return x, xc


def setup_inputs(seed: int = 0) -> dict:
    key = jax.random.key(seed)
    ks = jax.random.split(key, 26)
    f32 = jnp.float32
    L = DEPTH

    def nrm(k, shape, scale):
        return jax.random.normal(k, shape, f32) * scale

    def gain(k, shape):
        return 1.0 + nrm(k, shape, 0.02)

    gate_offset = jnp.concatenate([jnp.zeros((M_HEADS,), f32), jnp.full((M_HEADS,), FORGET_BIAS, f32)] * 2)
    return {
        "x": nrm(ks[0], (BATCH, SEQ, D_MODEL), 1.0),
        "c": nrm(ks[1], (BATCH, D_MODEL), 1.0),
        "ctx": nrm(ks[2], (BATCH, CTX_LEN, D_MODEL), 1.0),
        "c_ctx": nrm(ks[3], (D_MODEL,), 1.0),
        "w_ada": nrm(ks[4], (L, D_MODEL, N_MOD * D_MODEL), 0.5 * D_MODEL ** -0.5),
        "b_ada": nrm(ks[5], (L, N_MOD * D_MODEL), 0.02),
        "norm_pre_mix": gain(ks[6], (L, D_MODEL)),
        "norm_post_mix": gain(ks[7], (L, D_MODEL)),
        "norm_pre_ffn": gain(ks[8], (L, D_MODEL)),
        "norm_post_ffn": gain(ks[9], (L, D_MODEL)),
        "w_in": nrm(ks[10], (L, D_MODEL, D_IN), D_MODEL ** -0.5),
        "b_gates": gate_offset + nrm(ks[11], (L, N_GATE_COLS), 0.1),
        "m_out_norm": gain(ks[12], (L, M_V_W)),
        "q_norm": gain(ks[13], (L, Q_LORA)),
        "kv_norm": gain(ks[14], (L, KV_LORA)),
        "w_uq": nrm(ks[15], (L, Q_LORA, A_HEADS * (A_NOPE + A_ROPE)), Q_LORA ** -0.5),
        "w_ukv": nrm(ks[16], (L, KV_LORA, A_HEADS * (A_NOPE + A_DV)), KV_LORA ** -0.5),
        "w_branch_a": nrm(ks[17], (L, M_V_W, D_MODEL), M_V_W ** -0.5),
        "w_branch_b": nrm(ks[18], (L, A_V_W, D_MODEL), A_V_W ** -0.5),
        "w_out": nrm(ks[19], (L, D_MODEL, D_MODEL), D_MODEL ** -0.5),
        "router_w": nrm(ks[20], (L, D_MODEL, N_EXPERTS), D_MODEL ** -0.5),
        "router_b": nrm(ks[21], (L, N_EXPERTS), 0.01),
        "w_gu": nrm(ks[22], (L, N_EXPERTS, D_MODEL, 2 * D_EXPERT), D_MODEL ** -0.5),
        "b_gu": nrm(ks[23], (L, N_EXPERTS, 2 * D_EXPERT), 0.02),
        "w_down": nrm(ks[24], (L, N_EXPERTS, D_EXPERT, D_MODEL), D_EXPERT ** -0.5),
        "b_down": nrm(ks[25], (L, N_EXPERTS, D_MODEL), 0.02),
    }


def reference(x, c, ctx, c_ctx, w_ada, b_ada, norm_pre_mix, norm_post_mix, norm_pre_ffn, norm_post_ffn,
              w_in, b_gates, m_out_norm, q_norm, kv_norm, w_uq, w_ukv, w_branch_a, w_branch_b, w_out,
              router_w, router_b, w_gu, b_gu, w_down, b_down):
    ROWS = x.shape[1] // GRID_W
    cos, sin = axial_rope_tables(ROWS, x.dtype)
    c_silu = jax.nn.silu(c)
    cc_silu = jax.nn.silu(c_ctx)
    xc = ctx
    for l in range(DEPTH):
        x, xc = trunk_layer(x, xc, c_silu, cc_silu, cos, sin, w_ada[l], b_ada[l], norm_pre_mix[l], norm_post_mix[l],
                            norm_pre_ffn[l], norm_post_ffn[l], w_in[l], b_gates[l], m_out_norm[l], q_norm[l],
                            kv_norm[l], w_uq[l], w_ukv[l], w_branch_a[l], w_branch_b[l], w_out[l], router_w[l],
                            router_b[l], w_gu[l], b_gu[l], w_down[l], b_down[l], with_ctx_out=(l < DEPTH - 1))
    return x
```

```python
import functools

import jax
import jax.numpy as jnp
import numpy as np
from jax import lax
from jax.experimental import pallas as pl
from jax.experimental.pallas import tpu as pltpu

F32 = jnp.float32
BF16 = jnp.bfloat16

D_MODEL = 4096
SEQ = 2048
GRID_W = 64
CTX_LEN = 256
ROWS_ALL = SEQ + CTX_LEN
NORM_EPS = 1e-6
N_MOD = 6

M_HEADS = 8
M_DQK = 256
M_DV = 512
M_QK_W = M_HEADS * M_DQK
M_V_W = M_HEADS * M_DV
N_GATE_COLS = 4 * M_HEADS
M_CHUNK = 256

A_HEADS = 32
A_NOPE = 128
A_ROPE = 64
A_DV = 128
Q_LORA = 1024
KV_LORA = 512
ROPE_BASE = 10000.0
A_SCALE = (A_NOPE + A_ROPE) ** -0.5
A_QW = 256

N_EXPERTS = 32
TOP_K = 4
D_EXPERT = 1536
SWIGLU_LIMIT = 7.0
SWIGLU_ALPHA = 1.702
MOE_TILE = 256

OFF_Q = 0
OFF_G = 2 * M_QK_W + 2 * M_V_W
OFF_CQ = OFF_G + N_GATE_COLS
OFF_CKV = OFF_CQ + Q_LORA
OFF_KR = OFF_CKV + KV_LORA
OFF_GA = OFF_KR + A_ROPE
OFF_GB = OFF_GA + D_MODEL

LANES = 128
V7X_VMEM_LIMIT = 56 * 1024 * 1024


def _params(sem, vmem=V7X_VMEM_LIMIT):
    return pltpu.CompilerParams(dimension_semantics=sem, vmem_limit_bytes=vmem)


def _split3(x):
    hi = x.astype(BF16)
    r1 = x - hi.astype(F32)
    mid = r1.astype(BF16)
    lo = (r1 - mid.astype(F32)).astype(BF16)
    return hi, mid, lo


def _ada_kernel(c_ref, w_ref, b_ref, o_ref):
    c = c_ref[...]
    cs = c * jax.nn.sigmoid(c)
    c_hi, c_mid, _ = _split3(cs)
    w = w_ref[...]
    w_hi = w.astype(BF16)
    w_lo = (w - w_hi.astype(F32)).astype(BF16)
    acc = jnp.dot(c_hi, w_hi, preferred_element_type=F32)
    acc += jnp.dot(c_mid, w_hi, preferred_element_type=F32)
    acc += jnp.dot(c_hi, w_lo, preferred_element_type=F32)
    o_ref[...] = acc + b_ref[...]


def _ada(cvec, w_ada, b_ada):
    n = w_ada.shape[1]
    tn = 1024
    return pl.pallas_call(
        _ada_kernel,
        grid=(n // tn,),
        in_specs=[pl.BlockSpec((8, D_MODEL), lambda j: (0, 0)),
                  pl.BlockSpec((D_MODEL, tn), lambda j: (0, j)),
                  pl.BlockSpec((1, tn), lambda j: (0, j))],
        out_specs=pl.BlockSpec((8, tn), lambda j: (0, j)),
        out_shape=jax.ShapeDtypeStruct((8, n), F32),
        compiler_params=_params(("arbitrary",)),
        name="ada_mod",
    )(cvec, w_ada, b_ada.reshape(1, n))


def _prenorm_kernel(x_ref, c_ref, g_ref, sc_ref, sh_ref, o_ref):
    j = pl.program_id(1)
    nlat = SEQ // CTX_LEN

    def run(src):
        xf = src[0]
        ms = jnp.mean(xf * xf, axis=-1, keepdims=True)
        y = xf * lax.rsqrt(ms + NORM_EPS) * g_ref[...]
        o_ref[0] = (y * (1.0 + sc_ref[0, 0]) + sh_ref[0, 0]).astype(o_ref.dtype)

    @pl.when(j < nlat)
    def _():
        run(x_ref)

    @pl.when(j == nlat)
    def _():
        run(c_ref)


def _prenorm(x, ctx, gain, mod6):
    B = x.shape[0]
    nlat = SEQ // CTX_LEN
    blk = (1, CTX_LEN, D_MODEL)

    def mod_spec(k):
        return pl.BlockSpec((1, 1, 1, D_MODEL),
                            lambda b, j: (k, jnp.where(j == nlat, B, b), 0, 0))

    return pl.pallas_call(
        _prenorm_kernel,
        grid=(B, nlat + 1),
        in_specs=[pl.BlockSpec(blk, lambda b, j: (b, jnp.minimum(j, nlat - 1), 0)),
                  pl.BlockSpec(blk, lambda b, j: (b, 0, 0)),
                  pl.BlockSpec((1, D_MODEL), lambda b, j: (0, 0)),
                  mod_spec(1), mod_spec(0)],
        out_specs=pl.BlockSpec(blk, lambda b, j: (b, j, 0)),
        out_shape=jax.ShapeDtypeStruct((B, ROWS_ALL, D_MODEL), BF16),
        compiler_params=_params(("arbitrary", "arbitrary")),
        name="prenorm",
    )(x, ctx, gain.reshape(1, D_MODEL), mod6, mod6)


def _mm_kernel(*refs, rms, gate, add, rope, scale):
    refs = list(refs)
    x_ref = refs.pop(0)
    w_ref = refs.pop(0)
    gain_ref = refs.pop(0) if rms else None
    gate_ref = refs.pop(0) if gate else None
    add_ref = refs.pop(0) if add else None
    tab_ref = refs.pop(0) if rope else None
    o_ref = refs.pop(0)
    xn_ref = refs.pop(0) if rms else None

    if rms:
        @pl.when(pl.program_id(2) == 0)
        def _():
            xf = x_ref[0].astype(F32)
            ms = jnp.mean(xf * xf, axis=-1, keepdims=True)
            xn_ref[...] = (xf * lax.rsqrt(ms + NORM_EPS) * gain_ref[...]).astype(BF16)
        xv = xn_ref[...]
    else:
        xv = x_ref[0]
    acc = jnp.dot(xv, w_ref[...].astype(BF16), preferred_element_type=F32)
    if gate:
        acc = jax.nn.sigmoid(gate_ref[0].astype(F32)) * acc
    if add:
        acc = acc + add_ref[0].astype(F32)
    if rope:
        tab = tab_ref[0]
        for hh in range(acc.shape[1] // A_QW):
            c0 = hh * A_QW
            r = acc[:, c0 + LANES:c0 + A_QW] * tab
            r = r + pltpu.roll(r, A_ROPE, axis=1)
            o_ref[0, :, c0:c0 + LANES] = (acc[:, c0:c0 + LANES] * scale).astype(o_ref.dtype)
            o_ref[0, :, c0 + LANES:c0 + A_QW] = (r * scale).astype(o_ref.dtype)
    else:
        o_ref[0] = acc.astype(o_ref.dtype)


def _matmul(x, w, *, rows, tm, tn, n_out, out_dtype, x_col=0, k=None, w_col=0,
            rms_gain=None, gate=None, gate_col=0, add=None, rope_tab=None, scale=1.0, name="mm"):
    B = x.shape[0]
    k = x.shape[2] if k is None else k
    assert rows % tm == 0 and n_out % tn == 0 and w.shape[0] == k
    in_specs = [pl.BlockSpec((1, tm, k), lambda b, i, j: (b, i, x_col)),
                pl.BlockSpec((k, tn), lambda b, i, j: (0, j + w_col))]
    args = [x, w]
    scratch = []
    if rms_gain is not None:
        in_specs.append(pl.BlockSpec((1, k), lambda b, i, j: (0, 0)))
        args.append(rms_gain.reshape(1, k))
        scratch.append(pltpu.VMEM((tm, k), BF16))
    if gate is not None:
        in_specs.append(pl.BlockSpec((1, tm, tn), lambda b, i, j: (b, i, j + gate_col)))
        args.append(gate)
    if add is not None:
        in_specs.append(pl.BlockSpec((1, tm, tn), lambda b, i, j: (b, i, j)))
        args.append(add)
    if rope_tab is not None:
        in_specs.append(pl.BlockSpec((1, tm, LANES), lambda b, i, j: (0, i, 0)))
        args.append(rope_tab)
    kern = functools.partial(_mm_kernel, rms=rms_gain is not None, gate=gate is not None,
                             add=add is not None, rope=rope_tab is not None, scale=scale)
    return pl.pallas_call(
        kern,
        grid=(B, rows // tm, n_out // tn),
        in_specs=in_specs,
        out_specs=pl.BlockSpec((1, tm, tn), lambda b, i, j: (b, i, j)),
        out_shape=jax.ShapeDtypeStruct((B, rows, n_out), out_dtype),
        scratch_shapes=scratch,
        compiler_params=_params(("arbitrary", "arbitrary", "arbitrary")),
        name=name,
    )(*args)


def _small_kernel(x_ref, whi_ref, wlo_ref, bias_ref, tab_ref, zg_ref, kr_ref):
    xv = x_ref[0]
    acc = jnp.dot(xv, whi_ref[...], preferred_element_type=F32)
    acc += jnp.dot(xv, wlo_ref[...], preferred_element_type=F32)
    zg_ref[0] = acc[:, :LANES] + bias_ref[...]
    r = acc[:, LANES:] * tab_ref[0]
    r = r + pltpu.roll(r, A_ROPE, axis=1)
    lane = lax.broadcasted_iota(jnp.int32, r.shape, 1)
    kr_ref[0] = jnp.where(lane < A_ROPE, r, 0.0).astype(kr_ref.dtype)


def _small_proj(h_all, w_hi, w_lo, bias, tab, tm):
    B = h_all.shape[0]
    blk = lambda c: pl.BlockSpec((1, tm, c), lambda b, i: (b, i, 0))
    return pl.pallas_call(
        _small_kernel,
        grid=(B, ROWS_ALL // tm),
        in_specs=[blk(D_MODEL),
                  pl.BlockSpec((D_MODEL, 2 * LANES), lambda b, i: (0, 0)),
                  pl.BlockSpec((D_MODEL, 2 * LANES), lambda b, i: (0, 0)),
                  pl.BlockSpec((1, LANES), lambda b, i: (0, 0)),
                  pl.BlockSpec((1, tm, LANES), lambda b, i: (0, i, 0))],
        out_specs=[blk(LANES), blk(LANES)],
        out_shape=[jax.ShapeDtypeStruct((B, ROWS_ALL, LANES), F32),
                   jax.ShapeDtypeStruct((B, ROWS_ALL, LANES), BF16)],
        compiler_params=_params(("arbitrary", "arbitrary")),
        name="small_proj",
    )(h_all, w_hi, w_lo, bias, tab)


def _log_sigmoid(x):
    return jnp.minimum(x, 0.0) - jnp.log1p(jnp.exp(-jnp.abs(x)))


def _mlstm_dir(q_ref, k_ref, v_ref, zg_ref, o_ref, c_ref, n_ref, m_ref, *, reverse, head, is_ctx):
    L = M_CHUNK
    q = q_ref[0]
    k = k_ref[0]
    v = v_ref[0]
    zg = zg_ref[0]
    col_i = head + (2 * M_HEADS if reverse else 0)
    col_f = col_i + M_HEADS
    lane = lax.broadcasted_iota(jnp.int32, (L, LANES), 1)
    row_id = lax.broadcasted_iota(jnp.int32, (L, L), 0)
    col_id = lax.broadcasted_iota(jnp.int32, (L, L), 1)
    seen = (col_id >= row_id) if reverse else (col_id <= row_id)

    lf = _log_sigmoid(zg)
    tri = jnp.where(seen, 1.0, 0.0).astype(BF16)
    hi, mid, lo = _split3(lf)
    bcum = (jnp.dot(tri, hi, preferred_element_type=F32)
            + jnp.dot(tri, mid, preferred_element_type=F32)
            + jnp.dot(tri, lo, preferred_element_type=F32))

    def pick_col(a, c):
        return jnp.sum(jnp.where(lane == c, a, 0.0), axis=1, keepdims=True)

    def pick_row(a_t, c):
        sub = lax.broadcasted_iota(jnp.int32, (LANES, L), 0)
        return jnp.sum(jnp.where(sub == c, a_t, 0.0), axis=0, keepdims=True)

    b_col = pick_col(bcum, col_f)
    i_col = pick_col(zg, col_i)
    m_prev = m_ref[...]
    b_last = b_col[0:1, :] if reverse else b_col[L - 1:L, :]

    @pl.when(jnp.logical_not(is_ctx))
    def _():
        b_row = pick_row(bcum.T, col_f)
        i_row = pick_row(zg.T, col_i)
        d = jnp.where(seen, b_col - b_row + i_row, -jnp.inf)
        inter = b_col + m_prev
        m_t = jnp.maximum(inter, jnp.max(d, axis=1, keepdims=True))
        w = jnp.exp(d - m_t) * (M_DQK ** -0.5)
        scale = jnp.exp(inter - m_t)
        s = lax.dot_general(q, k, (((1,), (1,)), ((), ())), preferred_element_type=F32) * w
        cq = lax.dot_general(q, c_ref[...].astype(BF16), (((1,), (1,)), ((), ())),
                             preferred_element_type=F32)
        num = jnp.dot(s.astype(BF16), v, preferred_element_type=F32) + scale * cq
        nq = jnp.sum(q.astype(F32) * n_ref[...], axis=1, keepdims=True)
        den = jnp.sum(s, axis=1, keepdims=True) + scale * nq
        o_ref[0] = (num / jnp.maximum(jnp.abs(den), jnp.exp(-m_t))).astype(o_ref.dtype)

    g = b_last - b_col + i_col
    m_new = jnp.maximum(b_last + m_prev, jnp.max(g, axis=0, keepdims=True))
    wk = jnp.exp(g - m_new) * (M_DQK ** -0.5)
    dec = jnp.exp(b_last + m_prev - m_new)
    vw = (v.astype(F32) * wk).astype(BF16)
    upd = lax.dot_general(vw, k, (((0,), (0,)), ((), ())), preferred_element_type=F32)
    c_ref[...] = dec * c_ref[...] + upd
    n_ref[...] = dec * n_ref[...] + jnp.sum(k.astype(F32) * wk, axis=0, keepdims=True)
    m_ref[...] = m_new


def _mlstm_kernel(qf, kf, vf, gf, qb, kb, vb, gb, of, ob, cf, nf, mf, cb, nb, mb):
    head = pl.program_id(1)
    j = pl.program_id(2)

    @pl.when(j == 0)
    def _():
        for r in (cf, nf, mf, cb, nb, mb):
            r[...] = jnp.zeros_like(r)

    _mlstm_dir(qf, kf, vf, gf, of, cf, nf, mf, reverse=False, head=head, is_ctx=j == 0)
    _mlstm_dir(qb, kb, vb, gb, ob, cb, nb, mb, reverse=True, head=head, is_ctx=j == 0)


def _mlstm(z_a, zg):
    B = z_a.shape[0]
    L = M_CHUNK
    nlat = SEQ // L
    assert CTX_LEN == L
    v_blk0 = (2 * M_QK_W) // M_DV

    def cf(j):
        return jnp.where(j == 0, nlat, j - 1)

    def cb(j):
        return jnp.where(j == 0, nlat, nlat - j)

    def specs(ch):
        return [pl.BlockSpec((1, L, M_DQK), lambda b, h, j: (b, ch(j), h)),
                pl.BlockSpec((1, L, M_DQK), lambda b, h, j: (b, ch(j), M_HEADS + h)),
                pl.BlockSpec((1, L, M_DV), lambda b, h, j: (b, ch(j), v_blk0 + h)),
                pl.BlockSpec((1, L, LANES), lambda b, h, j: (b, ch(j), 0))]

    out_sds = jax.ShapeDtypeStruct((B, SEQ, M_V_W), BF16)
    state = [pltpu.VMEM((M_DV, M_DQK), F32), pltpu.VMEM((1, M_DQK), F32), pltpu.VMEM((1, 1), F32)]
    return pl.pallas_call(
        _mlstm_kernel,
        grid=(B, M_HEADS, nlat + 1),
        in_specs=specs(cf) + specs(cb),
        out_specs=[pl.BlockSpec((1, L, M_DV), lambda b, h, j: (b, jnp.maximum(j - 1, 0), h)),
                   pl.BlockSpec((1, L, M_DV), lambda b, h, j: (b, jnp.minimum(nlat - j, nlat - 1), h))],
        out_shape=[out_sds, out_sds],
        scratch_shapes=state + state,
        compiler_params=_params(("arbitrary", "arbitrary", "arbitrary")),
        name="mlstm_scan",
    )(z_a, z_a, z_a, zg, z_a, z_a, z_a, zg)


def _mlstm_out_kernel(hf_ref, hb_ref, zo_ref, g_ref, o_ref):
    for h in range(M_HEADS):
        sl = slice(h * M_DV, (h + 1) * M_DV)
        hs = hf_ref[0, :, sl].astype(F32) + hb_ref[0, :, sl].astype(F32)
        ms = jnp.mean(hs * hs, axis=-1, keepdims=True)
        hn = hs * lax.rsqrt(ms + NORM_EPS) * g_ref[:, sl]
        o_ref[0, :, sl] = (hn * jax.nn.sigmoid(zo_ref[0, :, sl].astype(F32))).astype(o_ref.dtype)


def _mlstm_out(h_f, h_b, z_a, gain, tm=256):
    B = h_f.shape[0]
    blk = pl.BlockSpec((1, tm, M_V_W), lambda b, i: (b, i, 0))
    return pl.pallas_call(
        _mlstm_out_kernel,
        grid=(B, SEQ // tm),
        in_specs=[blk, blk,
                  pl.BlockSpec((1, tm, M_V_W), lambda b, i: (b, i, (2 * M_QK_W + M_V_W) // M_V_W)),
                  pl.BlockSpec((1, M_V_W), lambda b, i: (0, 0))],
        out_specs=blk,
        out_shape=jax.ShapeDtypeStruct((B, SEQ, M_V_W), BF16),
        compiler_params=_params(("arbitrary", "arbitrary")),
        name="mlstm_out",
    )(h_f, h_b, z_a, gain.reshape(1, M_V_W))


def _attn_kernel(q_ref, kn_ref, kr_ref, v_ref, o_ref, *, sub):
    kcat = jnp.concatenate([kn_ref[0], kr_ref[0]], axis=1)
    v = v_ref[0]

    def body(r, carry):
        rows = pl.ds(pl.multiple_of(r * sub, sub), sub)
        q = q_ref[0, rows, :]
        s = lax.dot_general(q, kcat, (((1,), (1,)), ((), ())), preferred_element_type=F32)
        p = jnp.exp(s - jnp.max(s, axis=1, keepdims=True))
        l = jnp.sum(p, axis=1, keepdims=True)
        o = jnp.dot(p.astype(BF16), v, preferred_element_type=F32)
        o_ref[0, rows, :] = (o / l).astype(o_ref.dtype)
        return carry

    lax.fori_loop(0, q_ref.shape[1] // sub, body, 0)


def _attention(q, kv, krz, tq=1024, sub=256):
    B = q.shape[0]
    return pl.pallas_call(
        functools.partial(_attn_kernel, sub=sub),
        grid=(B, A_HEADS, SEQ // tq),
        in_specs=[pl.BlockSpec((1, tq, A_QW), lambda b, h, i: (b, i, h)),
                  pl.BlockSpec((1, ROWS_ALL, A_NOPE), lambda b, h, i: (b, 0, 2 * h)),
                  pl.BlockSpec((1, ROWS_ALL, LANES), lambda b, h, i: (b, 0, 0)),
                  pl.BlockSpec((1, ROWS_ALL, A_DV), lambda b, h, i: (b, 0, 2 * h + 1))],
        out_specs=pl.BlockSpec((1, tq, A_DV), lambda b, h, i: (b, i, h)),
        out_shape=jax.ShapeDtypeStruct((B, SEQ, A_HEADS * A_DV), BF16),
        compiler_params=_params(("arbitrary", "arbitrary", "arbitrary")),
        name="latent_attn",
    )(q, kv, krz, kv)


def _pack_bf16_pair(a, b):
    ua = pltpu.bitcast(a.astype(BF16).astype(F32), jnp.uint32)
    ub = pltpu.bitcast(b.astype(BF16).astype(F32), jnp.uint32)
    return (ua >> 16) | (ub & jnp.uint32(0xFFFF0000))


def _unpack_bf16_pair(u):
    a = pltpu.bitcast(u << 16, F32).astype(BF16)
    b = pltpu.bitcast(u & jnp.uint32(0xFFFF0000), F32).astype(BF16)
    return a, b


def _post1_kernel(x_ref, y_ref, gpost_ref, gpre_ref, g1_ref, sc_ref, sh_ref, rwh_ref, rwl_ref, rb_ref,
                  x1_ref, h2_ref, eidx_ref, gates_ref, rank_ref, cnt_ref, carry_ref):
    first = jnp.logical_and(pl.program_id(0) == 0, pl.program_id(1) == 0)

    @pl.when(first)
    def _():
        carry_ref[...] = jnp.zeros_like(carry_ref)

    y = y_ref[0].astype(F32)
    yn = y * lax.rsqrt(jnp.mean(y * y, axis=-1, keepdims=True) + NORM_EPS) * gpost_ref[...]
    x1 = x_ref[0] + g1_ref[0, 0] * yn
    x1_ref[0] = x1
    hn = x1 * lax.rsqrt(jnp.mean(x1 * x1, axis=-1, keepdims=True) + NORM_EPS) * gpre_ref[...]
    h2 = hn * (1.0 + sc_ref[0, 0]) + sh_ref[0, 0]
    half = D_MODEL // 2
    h2_ref[0] = _pack_bf16_pair(h2[:, :half], h2[:, half:])

    h_hi = h2.astype(BF16)
    h_lo = (h2 - h_hi.astype(F32)).astype(BF16)
    logits = (jnp.dot(h_hi, rwh_ref[...], preferred_element_type=F32)
              + jnp.dot(h_lo, rwh_ref[...], preferred_element_type=F32)
              + jnp.dot(h_hi, rwl_ref[...], preferred_element_type=F32)) + rb_ref[...]
    tm = logits.shape[0]
    lane = lax.broadcasted_iota(jnp.int32, (tm, LANES), 1)
    lane_f = lane.astype(F32)
    work = jnp.where(lane < N_EXPERTS, logits, -jnp.inf)
    sel = jnp.zeros((tm, LANES), F32)
    tops, hots = [], []
    for _ in range(TOP_K):
        mx = jnp.max(work, axis=1, keepdims=True)
        idx = jnp.min(jnp.where(work == mx, lane_f, float(LANES)), axis=1, keepdims=True)
        hot = lane_f == idx
        tops.append(mx)
        hots.append(hot)
        sel = jnp.where(hot, 1.0, sel)
        work = jnp.where(hot, -jnp.inf, work)
    es = [jnp.exp(t - tops[0]) for t in tops]
    tot = es[0] + es[1] + es[2] + es[3]

    r_id = lax.broadcasted_iota(jnp.int32, (tm, tm), 0)
    c_id = lax.broadcasted_iota(jnp.int32, (tm, tm), 1)
    strict = jnp.where(c_id < r_id, 1.0, 0.0).astype(BF16)
    rank_all = jnp.dot(strict, sel.astype(BF16), preferred_element_type=F32) + carry_ref[...]
    carry_ref[...] = carry_ref[...] + jnp.sum(sel, axis=0, keepdims=True)
    cnt_ref[...] = carry_ref[...]

    eidx = jnp.zeros((tm, LANES), jnp.int32)
    gates = jnp.zeros((tm, LANES), F32)
    rank = jnp.zeros((tm, LANES), jnp.int32)
    for kk in range(TOP_K):
        e_k = jnp.sum(jnp.where(hots[kk], lane_f, 0.0), axis=1, keepdims=True)
        r_k = jnp.sum(jnp.where(hots[kk], rank_all, 0.0), axis=1, keepdims=True)
        eidx = jnp.where(lane == kk, e_k.astype(jnp.int32), eidx)
        gates = jnp.where(lane == kk, es[kk] / tot, gates)
        rank = jnp.where(lane == kk, r_k.astype(jnp.int32), rank)
    eidx_ref[0] = eidx
    gates_ref[0] = gates
    rank_ref[0] = rank


def _post1(x, y, gpost, gpre, mod6, rw_hi, rw_lo, rb, tm=256):
    B = x.shape[0]
    blk = pl.BlockSpec((1, tm, D_MODEL), lambda b, i: (b, i, 0))
    vec = pl.BlockSpec((1, D_MODEL), lambda b, i: (0, 0))
    lan = pl.BlockSpec((1, tm, LANES), lambda b, i: (b, i, 0))
    rw = pl.BlockSpec((D_MODEL, LANES), lambda b, i: (0, 0))

    def mod_spec(k):
        return pl.BlockSpec((1, 1, 1, D_MODEL), lambda b, i: (k, b, 0, 0))

    lan_i = jax.ShapeDtypeStruct((B, SEQ, LANES), jnp.int32)
    return pl.pallas_call(
        _post1_kernel,
        grid=(B, SEQ // tm),
        in_specs=[blk, blk, vec, vec, mod_spec(2), mod_spec(4), mod_spec(3), rw, rw,
                  pl.BlockSpec((1, LANES), lambda b, i: (0, 0))],
        out_specs=[blk, pl.BlockSpec((1, tm, D_MODEL // 2), lambda b, i: (b, i, 0)), lan, lan, lan,
                   pl.BlockSpec((1, LANES), lambda b, i: (0, 0))],
        out_shape=[jax.ShapeDtypeStruct((B, SEQ, D_MODEL), F32),
                   jax.ShapeDtypeStruct((B, SEQ, D_MODEL // 2), jnp.uint32),
                   lan_i, jax.ShapeDtypeStruct((B, SEQ, LANES), F32), lan_i,
                   jax.ShapeDtypeStruct((1, LANES), F32)],
        scratch_shapes=[pltpu.VMEM((1, LANES), F32)],
        compiler_params=_params(("arbitrary", "arbitrary")),
        name="post_mix",
    )(x, y, gpost.reshape(1, D_MODEL), gpre.reshape(1, D_MODEL), mod6, mod6, mod6, rw_hi, rw_lo, rb)


GATHER_ROWS = 1024


def _gather_kernel(idx_ref, src_ref, dst_ref, sem):
    base = pl.program_id(0) * GATHER_ROWS

    def row_copy(r):
        return pltpu.make_async_copy(src_ref.at[pl.ds(idx_ref[base + r], 1)],
                                     dst_ref.at[pl.ds(base + r, 1)], sem)

    def start(r, c):
        row_copy(r).start()
        return c

    def wait(r, c):
        row_copy(r).wait()
        return c

    lax.fori_loop(0, GATHER_ROWS, start, 0)
    lax.fori_loop(0, GATHER_ROWS, wait, 0)


def _gather_rows(slot_tok, src):
    n = slot_tok.shape[0]
    return pl.pallas_call(
        _gather_kernel,
        grid_spec=pltpu.PrefetchScalarGridSpec(
            num_scalar_prefetch=1,
            grid=(n // GATHER_ROWS,),
            in_specs=[pl.BlockSpec(memory_space=pl.ANY)],
            out_specs=pl.BlockSpec(memory_space=pl.ANY),
            scratch_shapes=[pltpu.SemaphoreType.DMA]),
        out_shape=jax.ShapeDtypeStruct((n, src.shape[1]), src.dtype),
        compiler_params=_params(("arbitrary",)),
        name="moe_gather",
    )(slot_tok, src)


def _gmm1_kernel(te_ref, tv_ref, x_ref, wg_ref, wu_ref, bg_ref, bu_ref, o_ref, wgb_ref, wub_ref):
    i = pl.program_id(1)
    prev = te_ref[jnp.maximum(i - 1, 0)]
    fresh = jnp.logical_or(i == 0, te_ref[i] != prev)

    @pl.when(fresh)
    def _():
        wgb_ref[...] = wg_ref[0].astype(BF16)
        wub_ref[...] = wu_ref[0].astype(BF16)

    @pl.when(tv_ref[i] > 0)
    def _():
        lo, hi = _unpack_bf16_pair(x_ref[...])
        half = D_MODEL // 2
        glu = (jnp.dot(lo, wgb_ref[:half, :], preferred_element_type=F32)
               + jnp.dot(hi, wgb_ref[half:, :], preferred_element_type=F32) + bg_ref[0])
        lin = (jnp.dot(lo, wub_ref[:half, :], preferred_element_type=F32)
               + jnp.dot(hi, wub_ref[half:, :], preferred_element_type=F32) + bu_ref[0])
        glu = jnp.minimum(glu, SWIGLU_LIMIT)
        lin = jnp.clip(lin, -SWIGLU_LIMIT, SWIGLU_LIMIT)
        o_ref[...] = (glu * jax.nn.sigmoid(SWIGLU_ALPHA * glu) * (lin + 1.0)).astype(o_ref.dtype)

    @pl.when(tv_ref[i] == 0)
    def _():
        o_ref[...] = jnp.zeros_like(o_ref)


def _gmm1(tile_e, tile_v, xs, w_gu, b_gu, tn=512):
    n_slots = xs.shape[0]
    n_tiles = n_slots // MOE_TILE
    nf = D_EXPERT // tn
    return pl.pallas_call(
        _gmm1_kernel,
        grid_spec=pltpu.PrefetchScalarGridSpec(
            num_scalar_prefetch=2,
            grid=(nf, n_tiles),
            in_specs=[pl.BlockSpec((MOE_TILE, D_MODEL // 2), lambda n, i, te, tv: (i, 0)),
                      pl.BlockSpec((1, D_MODEL, tn), lambda n, i, te, tv: (te[i], 0, n)),
                      pl.BlockSpec((1, D_MODEL, tn), lambda n, i, te, tv: (te[i], 0, nf + n)),
                      pl.BlockSpec((1, 1, tn), lambda n, i, te, tv: (te[i], 0, n)),
                      pl.BlockSpec((1, 1, tn), lambda n, i, te, tv: (te[i], 0, nf + n))],
            out_specs=pl.BlockSpec((MOE_TILE, tn), lambda n, i, te, tv: (i, n)),
            scratch_shapes=[pltpu.VMEM((D_MODEL, tn), BF16), pltpu.VMEM((D_MODEL, tn), BF16)]),
        out_shape=jax.ShapeDtypeStruct((n_slots, D_EXPERT), BF16),
        compiler_params=_params(("arbitrary", "arbitrary")),
        name="moe_gate_up",
    )(tile_e, tile_v, xs, w_gu, w_gu, b_gu.reshape(N_EXPERTS, 1, 2 * D_EXPERT),
      b_gu.reshape(N_EXPERTS, 1, 2 * D_EXPERT))


def _gmm2_kernel(te_ref, tv_ref, a_ref, w_ref, b_ref, o_ref, wb_ref):
    i = pl.program_id(1)
    prev = te_ref[jnp.maximum(i - 1, 0)]
    fresh = jnp.logical_or(i == 0, te_ref[i] != prev)

    @pl.when(fresh)
    def _():
        wb_ref[...] = w_ref[0].astype(BF16)

    @pl.when(tv_ref[i] > 0)
    def _():
        y = jnp.dot(a_ref[...], wb_ref[...], preferred_element_type=F32) + b_ref[0]
        half = y.shape[1] // 2
        o_ref[...] = _pack_bf16_pair(y[:, :half], y[:, half:])

    @pl.when(tv_ref[i] == 0)
    def _():
        o_ref[...] = jnp.zeros_like(o_ref)


def _gmm2(tile_e, tile_v, act, w_down, b_down, tn=1024):
    n_slots = act.shape[0]
    n_tiles = n_slots // MOE_TILE
    nn = D_MODEL // tn
    return pl.pallas_call(
        _gmm2_kernel,
        grid_spec=pltpu.PrefetchScalarGridSpec(
            num_scalar_prefetch=2,
            grid=(nn, n_tiles),
            in_specs=[pl.BlockSpec((MOE_TILE, D_EXPERT), lambda n, i, te, tv: (i, 0)),
                      pl.BlockSpec((1, D_EXPERT, tn), lambda n, i, te, tv: (te[i], 0, n)),
                      pl.BlockSpec((1, 1, tn), lambda n, i, te, tv: (te[i], 0, n))],
            out_specs=pl.BlockSpec((MOE_TILE, tn // 2), lambda n, i, te, tv: (i, n)),
            scratch_shapes=[pltpu.VMEM((D_EXPERT, tn), BF16)]),
        out_shape=jax.ShapeDtypeStruct((n_slots, D_MODEL // 2), jnp.uint32),
        compiler_params=_params(("arbitrary", "arbitrary")),
        name="moe_down",
    )(tile_e, tile_v, act, w_down, b_down.reshape(N_EXPERTS, 1, D_MODEL))


COMBINE_TM = 256
GMM2_TN = 1024


def _combine_kernel(dest_ref, x1_ref, gates_ref, gpost_ref, g2_ref, ys_ref, o_ref, buf_ref, sem):
    b = pl.program_id(0)
    i = pl.program_id(1)
    tm = COMBINE_TM
    base = ((b * (SEQ // tm)) + i) * (tm * TOP_K)

    def row_copy(r):
        return pltpu.make_async_copy(ys_ref.at[pl.ds(dest_ref[base + r], 1)],
                                     buf_ref.at[pl.ds(r, 1)], sem)

    def start(r, c):
        row_copy(r).start()
        return c

    def wait(r, c):
        row_copy(r).wait()
        return c

    lax.fori_loop(0, tm * TOP_K, start, 0)
    lax.fori_loop(0, tm * TOP_K, wait, 0)

    gates = gates_ref[0]
    lane = lax.broadcasted_iota(jnp.int32, gates.shape, 1)
    half = GMM2_TN // 2
    parts = [None] * (2 * (D_MODEL // GMM2_TN))
    for kk in range(TOP_K):
        gk = jnp.sum(jnp.where(lane == kk, gates, 0.0), axis=1, keepdims=True)
        rows = buf_ref[kk * tm:(kk + 1) * tm, :]
        for n in range(D_MODEL // GMM2_TN):
            lo, hi = _unpack_bf16_pair(rows[:, n * half:(n + 1) * half])
            for t, val in ((2 * n, lo), (2 * n + 1, hi)):
                term = gk * val.astype(F32)
                parts[t] = term if parts[t] is None else parts[t] + term
    f = jnp.concatenate(parts, axis=1)
    fn = f * lax.rsqrt(jnp.mean(f * f, axis=-1, keepdims=True) + NORM_EPS) * gpost_ref[...]
    o_ref[0] = x1_ref[0] + g2_ref[0, 0] * fn


def _combine(dest, x1, gates, gpost, mod6, ys):
    B = x1.shape[0]
    tm = COMBINE_TM
    blk = pl.BlockSpec((1, tm, D_MODEL), lambda b, i, d: (b, i, 0))
    return pl.pallas_call(
        _combine_kernel,
        grid_spec=pltpu.PrefetchScalarGridSpec(
            num_scalar_prefetch=1,
            grid=(B, SEQ // tm),
            in_specs=[blk,
                      pl.BlockSpec((1, tm, LANES), lambda b, i, d: (b, i, 0)),
                      pl.BlockSpec((1, D_MODEL), lambda b, i, d: (0, 0)),
                      pl.BlockSpec((1, 1, 1, D_MODEL), lambda b, i, d: (5, b, 0, 0)),
                      pl.BlockSpec(memory_space=pl.ANY)],
            out_specs=blk,
            scratch_shapes=[pltpu.VMEM((tm * TOP_K, D_MODEL // 2), jnp.uint32),
                            pltpu.SemaphoreType.DMA]),
        out_shape=jax.ShapeDtypeStruct((B, SEQ, D_MODEL), F32),
        compiler_params=_params(("arbitrary", "arbitrary")),
        name="moe_combine",
    )(dest, x1, gates, gpost.reshape(1, D_MODEL), mod6, ys)


def _rope_table():
    rows = SEQ // GRID_W
    r, col = jnp.meshgrid(jnp.arange(rows, dtype=F32), jnp.arange(GRID_W, dtype=F32), indexing="ij")
    n_freq = A_ROPE // 4
    inv = ROPE_BASE ** (-jnp.arange(n_freq, dtype=F32) / n_freq)
    ang = jnp.concatenate([r.reshape(-1, 1) * inv, col.reshape(-1, 1) * inv], axis=-1)
    cos = jnp.repeat(jnp.cos(ang), 2, axis=-1)
    sin = jnp.repeat(jnp.sin(ang), 2, axis=-1)
    lat = jnp.concatenate([cos, sin], axis=-1)
    ctx = jnp.concatenate([jnp.ones((CTX_LEN, A_ROPE), F32), jnp.zeros((CTX_LEN, A_ROPE), F32)], axis=-1)
    return jnp.concatenate([lat, ctx], axis=0)[None]


def _rot_partner(w):
    wp = w.reshape(*w.shape[:-1], A_ROPE // 2, 2)
    return jnp.stack([-wp[..., 1], wp[..., 0]], axis=-1).reshape(w.shape)


def _layer(x, ctx, mod6, tab, norm_pre_mix, norm_post_mix, norm_pre_ffn, norm_post_ffn, w_in, b_gates,
           m_out_norm, q_norm, kv_norm, w_uq, w_ukv, w_branch_a, w_branch_b, w_out, router_w, router_b,
           w_gu, b_gu, w_down, b_down):
    B = x.shape[0]

    w_kr = w_in[:, OFF_KR:OFF_KR + A_ROPE]
    w_small = jnp.concatenate([w_in[:, OFF_G:OFF_G + N_GATE_COLS],
                               jnp.zeros((D_MODEL, LANES - N_GATE_COLS), F32), w_kr, _rot_partner(w_kr)], axis=1)
    ws_hi = w_small.astype(BF16)
    ws_lo = (w_small - ws_hi.astype(F32)).astype(BF16)
    gate_bias = jnp.concatenate([b_gates, jnp.zeros((LANES - N_GATE_COLS,), F32)]).reshape(1, LANES)
    w_lat = w_in[:, OFF_CQ:OFF_KR].astype(BF16)
    w_gab = w_in[:, OFF_GA:].astype(BF16)
    wq = w_uq.reshape(Q_LORA, A_HEADS, A_NOPE + A_ROPE)
    wq_r = wq[..., A_NOPE:]
    w_uq2 = jnp.concatenate([wq[..., :A_NOPE], wq_r, _rot_partner(wq_r)], axis=-1)
    w_uq2 = w_uq2.reshape(Q_LORA, A_HEADS * A_QW).astype(BF16)

    h_all = _prenorm(x, ctx, norm_pre_mix, mod6)

    z_a = _matmul(h_all, w_in, rows=ROWS_ALL, tm=1152, tn=512, n_out=OFF_G, out_dtype=BF16, name="in_proj_a")
    z_b = _matmul(h_all, w_lat, rows=ROWS_ALL, tm=1152, tn=512, n_out=Q_LORA + KV_LORA, out_dtype=BF16,
                  name="in_proj_b")
    z_d = _matmul(h_all, w_gab, rows=SEQ, tm=1024, tn=512, n_out=2 * D_MODEL, out_dtype=BF16, name="in_proj_d")
    zg, krz = _small_proj(h_all, ws_hi, ws_lo, gate_bias, tab, tm=1152)

    h_f, h_b = _mlstm(z_a, zg)
    ya = _mlstm_out(h_f, h_b, z_a, m_out_norm)

    q = _matmul(z_b, w_uq2, rows=SEQ, tm=1024, tn=1024, n_out=A_HEADS * A_QW, out_dtype=BF16, k=Q_LORA,
                rms_gain=q_norm, rope_tab=tab, scale=A_SCALE, name="q_up")
    kv = _matmul(z_b, w_ukv, rows=ROWS_ALL, tm=1152, tn=1024, n_out=A_HEADS * (A_NOPE + A_DV), out_dtype=BF16,
                 k=KV_LORA, x_col=Q_LORA // KV_LORA, rms_gain=kv_norm, name="kv_up")
    yb = _attention(q, kv, krz)

    ua = _matmul(ya, w_branch_a, rows=SEQ, tm=1024, tn=512, n_out=D_MODEL, out_dtype=BF16, gate=z_d,
                 name="branch_a")
    u = _matmul(yb, w_branch_b, rows=SEQ, tm=1024, tn=512, n_out=D_MODEL, out_dtype=BF16, gate=z_d,
                gate_col=D_MODEL // 512, add=ua, name="branch_b")
    y = _matmul(u, w_out, rows=SEQ, tm=1024, tn=512, n_out=D_MODEL, out_dtype=BF16, name="out_proj")

    rw = jnp.concatenate([router_w, jnp.zeros((D_MODEL, LANES - N_EXPERTS), F32)], axis=1)
    rw_hi = rw.astype(BF16)
    rw_lo = (rw - rw_hi.astype(F32)).astype(BF16)
    rb = jnp.concatenate([router_b, jnp.zeros((LANES - N_EXPERTS,), F32)]).reshape(1, LANES)
    x1, h2p, eidx, gates, rank, counts = _post1(x, y, norm_post_mix, norm_pre_ffn, mod6, rw_hi, rw_lo, rb)

    T = B * SEQ
    n_slots = T * TOP_K + N_EXPERTS * MOE_TILE
    n_tiles = n_slots // MOE_TILE
    cnt = counts[0, :N_EXPERTS].astype(jnp.int32)
    padded = (cnt + MOE_TILE - 1) // MOE_TILE * MOE_TILE
    g_end = jnp.cumsum(padded)
    g_start = g_end - padded
    e_flat = eidx.reshape(T, LANES)[:, :TOP_K]
    dest = g_start[e_flat] + rank.reshape(T, LANES)[:, :TOP_K]
    tok = jnp.broadcast_to(jnp.arange(T, dtype=jnp.int32)[:, None], (T, TOP_K))
    slot_tok = jnp.zeros((n_slots,), jnp.int32).at[dest.reshape(-1)].set(tok.reshape(-1))
    tile_lo = jnp.arange(n_tiles, dtype=jnp.int32) * MOE_TILE
    tile_e = jnp.minimum(jnp.searchsorted(g_end, tile_lo, side="right"), N_EXPERTS - 1).astype(jnp.int32)
    tile_v = (tile_lo < g_end[-1]).astype(jnp.int32)
    dest_t = dest.reshape(T // COMBINE_TM, COMBINE_TM, TOP_K).transpose(0, 2, 1).reshape(-1)

    xs = _gather_rows(slot_tok, h2p.reshape(T, D_MODEL // 2))
    act = _gmm1(tile_e, tile_v, xs, w_gu, b_gu)
    ys = _gmm2(tile_e, tile_v, act, w_down, b_down, tn=GMM2_TN)
    return _combine(dest_t, x1, gates, norm_post_ffn, mod6, ys)


def kernel(x, c, ctx, c_ctx, w_ada, b_ada, norm_pre_mix, norm_post_mix, norm_pre_ffn, norm_post_ffn, w_in,
           b_gates, m_out_norm, q_norm, kv_norm, w_uq, w_ukv, w_branch_a, w_branch_b, w_out, router_w, router_b,
           w_gu, b_gu, w_down, b_down):
    B = x.shape[0]
    depth = w_ada.shape[0]
    assert depth == 1, "context-stream outputs between layers are not implemented"
    tab = _rope_table()
    cvec = jnp.concatenate([c, c_ctx[None], jnp.zeros((8 - B - 1, D_MODEL), F32)], axis=0)
    l = 0
    mod = _ada(cvec, w_ada[l], b_ada[l])
    mod6 = mod.reshape(8, N_MOD, 1, D_MODEL).transpose(1, 0, 2, 3)
    return _layer(x, ctx, mod6, tab, norm_pre_mix[l], norm_post_mix[l], norm_pre_ffn[l], norm_post_ffn[l],
                  w_in[l], b_gates[l], m_out_norm[l], q_norm[l], kv_norm[l], w_uq[l], w_ukv[l], w_branch_a[l],
                  w_branch_b[l], w_out[l], router_w[l], router_b[l], w_gu[l], b_gu[l], w_down[l], b_down[l])
```

```python
import functools

import jax
import jax.numpy as jnp
import numpy as np
from jax import lax
from jax.experimental import pallas as pl
from jax.experimental.pallas import tpu as pltpu

F32 = jnp.float32
BF16 = jnp.bfloat16

D_MODEL = 4096
SEQ = 2048
GRID_W = 64
CTX_LEN = 256
ROWS_ALL = SEQ + CTX_LEN
NORM_EPS = 1e-6
N_MOD = 6

M_HEADS = 8
M_DQK = 256
M_DV = 512
M_QK_W = M_HEADS * M_DQK
M_V_W = M_HEADS * M_DV
N_GATE_COLS = 4 * M_HEADS
M_CHUNK = 256

A_HEADS = 32
A_NOPE = 128
A_ROPE = 64
A_DV = 128
Q_LORA = 1024
KV_LORA = 512
ROPE_BASE = 10000.0
A_SCALE = (A_NOPE + A_ROPE) ** -0.5
A_QW = 256

N_EXPERTS = 32
TOP_K = 4
D_EXPERT = 1536
SWIGLU_LIMIT = 7.0
SWIGLU_ALPHA = 1.702
MOE_TILE = 384

OFF_Q = 0
OFF_G = 2 * M_QK_W + 2 * M_V_W
OFF_CQ = OFF_G + N_GATE_COLS
OFF_CKV = OFF_CQ + Q_LORA
OFF_KR = OFF_CKV + KV_LORA
OFF_GA = OFF_KR + A_ROPE
OFF_GB = OFF_GA + D_MODEL

LANES = 128
V7X_VMEM_LIMIT = 56 * 1024 * 1024


def _params(sem, vmem=V7X_VMEM_LIMIT):
    return pltpu.CompilerParams(dimension_semantics=sem, vmem_limit_bytes=vmem)


def _split3(x):
    hi = x.astype(BF16)
    r1 = x - hi.astype(F32)
    mid = r1.astype(BF16)
    lo = (r1 - mid.astype(F32)).astype(BF16)
    return hi, mid, lo


def _ada_kernel(c_ref, w_ref, b_ref, o_ref):
    c = c_ref[...]
    cs = c * jax.nn.sigmoid(c)
    c_hi, c_mid, _ = _split3(cs)
    w = w_ref[...]
    w_hi = w.astype(BF16)
    w_lo = (w - w_hi.astype(F32)).astype(BF16)
    acc = jnp.dot(c_hi, w_hi, preferred_element_type=F32)
    acc += jnp.dot(c_mid, w_hi, preferred_element_type=F32)
    acc += jnp.dot(c_hi, w_lo, preferred_element_type=F32)
    o_ref[...] = acc + b_ref[...]


def _ada(cvec, w_ada, b_ada):
    n = w_ada.shape[1]
    tn = 1024
    return pl.pallas_call(
        _ada_kernel,
        grid=(n // tn,),
        in_specs=[pl.BlockSpec((8, D_MODEL), lambda j: (0, 0)),
                  pl.BlockSpec((D_MODEL, tn), lambda j: (0, j)),
                  pl.BlockSpec((1, tn), lambda j: (0, j))],
        out_specs=pl.BlockSpec((8, tn), lambda j: (0, j)),
        out_shape=jax.ShapeDtypeStruct((8, n), F32),
        compiler_params=_params(("arbitrary",)),
        name="ada_mod",
    )(cvec, w_ada, b_ada.reshape(1, n))


def _prenorm_kernel(x_ref, c_ref, g_ref, sc_ref, sh_ref, o_ref):
    j = pl.program_id(1)
    nlat = SEQ // CTX_LEN

    def run(src):
        xf = src[0]
        ms = jnp.mean(xf * xf, axis=-1, keepdims=True)
        y = xf * lax.rsqrt(ms + NORM_EPS) * g_ref[...]
        o_ref[0] = (y * (1.0 + sc_ref[0, 0]) + sh_ref[0, 0]).astype(o_ref.dtype)

    @pl.when(j < nlat)
    def _():
        run(x_ref)

    @pl.when(j == nlat)
    def _():
        run(c_ref)


def _prenorm(x, ctx, gain, mod6):
    B = x.shape[0]
    nlat = SEQ // CTX_LEN
    blk = (1, CTX_LEN, D_MODEL)

    def mod_spec(k):
        return pl.BlockSpec((1, 1, 1, D_MODEL),
                            lambda b, j: (k, jnp.where(j == nlat, B, b), 0, 0))

    return pl.pallas_call(
        _prenorm_kernel,
        grid=(B, nlat + 1),
        in_specs=[pl.BlockSpec(blk, lambda b, j: (b, jnp.minimum(j, nlat - 1), 0)),
                  pl.BlockSpec(blk, lambda b, j: (b, 0, 0)),
                  pl.BlockSpec((1, D_MODEL), lambda b, j: (0, 0)),
                  mod_spec(1), mod_spec(0)],
        out_specs=pl.BlockSpec(blk, lambda b, j: (b, j, 0)),
        out_shape=jax.ShapeDtypeStruct((B, ROWS_ALL, D_MODEL), BF16),
        compiler_params=_params(("arbitrary", "arbitrary")),
        name="prenorm",
    )(x, ctx, gain.reshape(1, D_MODEL), mod6, mod6)


def _mm_kernel(*refs, rms, gate, add, rope, scale, w_transposed):
    refs = list(refs)
    x_ref = refs.pop(0)
    w_ref = refs.pop(0)
    gain_ref = refs.pop(0) if rms else None
    gate_ref = refs.pop(0) if gate else None
    add_ref = refs.pop(0) if add else None
    tab_ref = refs.pop(0) if rope else None
    o_ref = refs.pop(0)
    xn_ref = refs.pop(0) if rms else None

    if rms:
        @pl.when(pl.program_id(2) == 0)
        def _():
            xf = x_ref[0].astype(F32)
            ms = jnp.mean(xf * xf, axis=-1, keepdims=True)
            xn_ref[...] = (xf * lax.rsqrt(ms + NORM_EPS) * gain_ref[...]).astype(BF16)
        xv = xn_ref[...]
    else:
        xv = x_ref[0]
    contract = (((1,), (1,)), ((), ())) if w_transposed else (((1,), (0,)), ((), ()))
    acc = lax.dot_general(xv, w_ref[...].astype(BF16), contract, preferred_element_type=F32)
    if gate:
        acc = jax.nn.sigmoid(gate_ref[0].astype(F32)) * acc
    if add:
        acc = acc + add_ref[0].astype(F32)
    if rope:
        tab = tab_ref[0]
        for hh in range(acc.shape[1] // A_QW):
            c0 = hh * A_QW
            r = acc[:, c0 + LANES:c0 + A_QW] * tab
            r = r + pltpu.roll(r, A_ROPE, axis=1)
            o_ref[0, :, c0:c0 + LANES] = (acc[:, c0:c0 + LANES] * scale).astype(o_ref.dtype)
            o_ref[0, :, c0 + LANES:c0 + A_QW] = (r * scale).astype(o_ref.dtype)
    else:
        o_ref[0] = acc.astype(o_ref.dtype)


def _matmul(x, w, *, rows, tm, tn, n_out, out_dtype, x_col=0, k=None, w_col=0, wt_row0=None,
            rms_gain=None, gate=None, gate_col=0, add=None, rope_tab=None, scale=1.0, name="mm"):
    B = x.shape[0]
    k = x.shape[2] if k is None else k
    assert rows % tm == 0 and n_out % tn == 0
    if wt_row0 is None:
        assert w.shape[0] == k
        w_spec = pl.BlockSpec((k, tn), lambda b, i, j: (0, j + w_col))
    else:
        assert w.shape[1] == k and wt_row0 % 8 == 0
        w_spec = pl.BlockSpec((pl.Element(tn), pl.Element(k)),
                              lambda b, i, j: (pl.multiple_of(wt_row0 + j * tn, 8), 0))
    in_specs = [pl.BlockSpec((1, tm, k), lambda b, i, j: (b, i, x_col)), w_spec]
    args = [x, w]
    scratch = []
    if rms_gain is not None:
        in_specs.append(pl.BlockSpec((1, k), lambda b, i, j: (0, 0)))
        args.append(rms_gain.reshape(1, k))
        scratch.append(pltpu.VMEM((tm, k), BF16))
    if gate is not None:
        in_specs.append(pl.BlockSpec((1, tm, tn), lambda b, i, j: (b, i, j + gate_col)))
        args.append(gate)
    if add is not None:
        in_specs.append(pl.BlockSpec((1, tm, tn), lambda b, i, j: (b, i, j)))
        args.append(add)
    if rope_tab is not None:
        in_specs.append(pl.BlockSpec((1, tm, LANES), lambda b, i, j: (0, i, 0)))
        args.append(rope_tab)
    kern = functools.partial(_mm_kernel, rms=rms_gain is not None, gate=gate is not None, add=add is not None,
                             rope=rope_tab is not None, scale=scale, w_transposed=wt_row0 is not None)
    return pl.pallas_call(
        kern,
        grid=(B, rows // tm, n_out // tn),
        in_specs=in_specs,
        out_specs=pl.BlockSpec((1, tm, tn), lambda b, i, j: (b, i, j)),
        out_shape=jax.ShapeDtypeStruct((B, rows, n_out), out_dtype),
        scratch_shapes=scratch,
        compiler_params=_params(("arbitrary", "arbitrary", "arbitrary")),
        name=name,
    )(*args)


def _small_kernel(x_ref, wg_ref, wk_ref, bias_ref, cos_ref, sin_ref, zg_ref, kr_ref):
    xv = x_ref[0]

    def proj(w_ref):
        w = w_ref[...]
        w_hi = w.astype(BF16)
        w_lo = (w - w_hi.astype(F32)).astype(BF16)
        nt = (((1,), (1,)), ((), ()))
        return (lax.dot_general(xv, w_hi, nt, preferred_element_type=F32)
                + lax.dot_general(xv, w_lo, nt, preferred_element_type=F32))

    zg_ref[0] = proj(wg_ref) + bias_ref[...]
    kr = proj(wk_ref)
    lane = lax.broadcasted_iota(jnp.int32, kr.shape, 1)
    partner = jnp.where((lane & 1) == 0, -pltpu.roll(kr, LANES - 1, axis=1), pltpu.roll(kr, 1, axis=1))
    kr_ref[0] = (kr * cos_ref[0] + partner * sin_ref[0]).astype(kr_ref.dtype)


def _small_proj(h_all, w_in_t, bias, cos_k, sin_k, tm):
    B = h_all.shape[0]
    blk = lambda c: pl.BlockSpec((1, tm, c), lambda b, i: (b, i, 0))
    tab = pl.BlockSpec((1, tm, LANES), lambda b, i: (0, i, 0))
    w_rows = lambda r0: pl.BlockSpec((pl.Element(LANES), pl.Element(D_MODEL)), lambda b, i: (r0, 0))
    return pl.pallas_call(
        _small_kernel,
        grid=(B, ROWS_ALL // tm),
        in_specs=[blk(D_MODEL), w_rows(OFF_G), w_rows(OFF_KR),
                  pl.BlockSpec((1, LANES), lambda b, i: (0, 0)),
                  tab, tab],
        out_specs=[blk(LANES), blk(LANES)],
        out_shape=[jax.ShapeDtypeStruct((B, ROWS_ALL, LANES), F32),
                   jax.ShapeDtypeStruct((B, ROWS_ALL, LANES), BF16)],
        compiler_params=_params(("arbitrary", "arbitrary")),
        name="small_proj",
    )(h_all, w_in_t, w_in_t, bias, cos_k, sin_k)


def _log_sigmoid(x):
    return jnp.minimum(x, 0.0) - jnp.log1p(jnp.exp(-jnp.abs(x)))


MLSTM_HP = 2


def _mlstm_dir(q_ref, k_ref, v_ref, zg_ref, o_ref, c_ref, n_ref, m_ref, *, reverse, head0):
    L = M_CHUNK
    zg = zg_ref[0]
    lane = lax.broadcasted_iota(jnp.int32, (L, LANES), 1)
    sub = lax.broadcasted_iota(jnp.int32, (LANES, L), 0)
    row_id = lax.broadcasted_iota(jnp.int32, (L, L), 0)
    col_id = lax.broadcasted_iota(jnp.int32, (L, L), 1)
    seen = (col_id >= row_id) if reverse else (col_id <= row_id)

    lf = _log_sigmoid(zg)
    tri = jnp.where(seen, 1.0, 0.0).astype(BF16)
    hi, mid, lo = _split3(lf)
    bcum = (jnp.dot(tri, hi, preferred_element_type=F32)
            + jnp.dot(tri, mid, preferred_element_type=F32)
            + jnp.dot(tri, lo, preferred_element_type=F32))

    bcum_t = bcum.T
    zg_t = zg.T

    def pick_col(a, c):
        return jnp.sum(jnp.where(lane == c, a, 0.0), axis=1, keepdims=True)

    def pick_row(a_t, c):
        return jnp.sum(jnp.where(sub == c, a_t, 0.0), axis=0, keepdims=True)

    for hh in range(MLSTM_HP):
        col_i = head0 + hh + (2 * M_HEADS if reverse else 0)
        col_f = col_i + M_HEADS
        q = q_ref[0, :, hh * M_DQK:(hh + 1) * M_DQK]
        k = k_ref[0, :, hh * M_DQK:(hh + 1) * M_DQK]
        v = v_ref[0, :, hh * M_DV:(hh + 1) * M_DV]
        c_prev = c_ref[hh]
        n_prev = n_ref[hh]
        m_prev = m_ref[hh]

        b_col = pick_col(bcum, col_f)
        i_col = pick_col(zg, col_i)
        b_row = pick_row(bcum_t, col_f)
        i_row = pick_row(zg_t, col_i)
        b_last = b_col[0:1, :] if reverse else b_col[L - 1:L, :]

        d = jnp.where(seen, b_col - b_row + i_row, -jnp.inf)
        inter = b_col + m_prev
        m_t = jnp.maximum(inter, jnp.max(d, axis=1, keepdims=True))
        w = jnp.exp(d - m_t) * (M_DQK ** -0.5)
        scale = jnp.exp(inter - m_t)
        s = lax.dot_general(q, k, (((1,), (1,)), ((), ())), preferred_element_type=F32) * w
        cq = lax.dot_general(q, c_prev.astype(BF16), (((1,), (1,)), ((), ())),
                             preferred_element_type=F32)
        num = jnp.dot(s.astype(BF16), v, preferred_element_type=F32) + scale * cq
        nq = jnp.sum(q.astype(F32) * n_prev, axis=1, keepdims=True)
        den = jnp.sum(s, axis=1, keepdims=True) + scale * nq
        h_out = num / jnp.maximum(jnp.abs(den), jnp.exp(-m_t))
        o_ref[0, :, hh * M_DV:(hh + 1) * M_DV] = h_out.astype(o_ref.dtype)

        g = b_last - b_col + i_col
        m_new = jnp.maximum(b_last + m_prev, jnp.max(g, axis=0, keepdims=True))
        wk = jnp.exp(g - m_new) * (M_DQK ** -0.5)
        dec = jnp.exp(b_last + m_prev - m_new)
        vw = (v.astype(F32) * wk).astype(BF16)
        upd = lax.dot_general(vw, k, (((0,), (0,)), ((), ())), preferred_element_type=F32)
        c_ref[hh] = dec * c_prev + upd
        n_ref[hh] = dec * n_prev + jnp.sum(k.astype(F32) * wk, axis=0, keepdims=True)
        m_ref[hh] = m_new


def _mlstm_kernel(qf, kf, vf, gf, qb, kb, vb, gb, of, ob, cf, nf, mf, cb, nb, mb):
    head0 = pl.program_id(1) * MLSTM_HP

    @pl.when(pl.program_id(2) == 0)
    def _():
        for r in (cf, nf, mf, cb, nb, mb):
            r[...] = jnp.zeros_like(r)

    _mlstm_dir(qf, kf, vf, gf, of, cf, nf, mf, reverse=False, head0=head0)
    _mlstm_dir(qb, kb, vb, gb, ob, cb, nb, mb, reverse=True, head0=head0)


def _mlstm(z_a, zg):
    B = z_a.shape[0]
    L = M_CHUNK
    HP = MLSTM_HP
    nlat = SEQ // L
    assert CTX_LEN == L and M_HEADS % HP == 0
    k_blk0 = M_QK_W // (HP * M_DQK)
    v_blk0 = (2 * M_QK_W) // (HP * M_DV)

    def cf(j):
        return jnp.where(j == 0, nlat, j - 1)

    def cb(j):
        return jnp.where(j == 0, nlat, nlat - j)

    def specs(ch):
        return [pl.BlockSpec((1, L, HP * M_DQK), lambda b, h, j: (b, ch(j), h)),
                pl.BlockSpec((1, L, HP * M_DQK), lambda b, h, j: (b, ch(j), k_blk0 + h)),
                pl.BlockSpec((1, L, HP * M_DV), lambda b, h, j: (b, ch(j), v_blk0 + h)),
                pl.BlockSpec((1, L, LANES), lambda b, h, j: (b, ch(j), 0))]

    out_sds = jax.ShapeDtypeStruct((B, SEQ, M_V_W), BF16)
    state = [pltpu.VMEM((HP, M_DV, M_DQK), F32), pltpu.VMEM((HP, 1, M_DQK), F32),
             pltpu.VMEM((HP, 1, 1), F32)]
    return pl.pallas_call(
        _mlstm_kernel,
        grid=(B, M_HEADS // HP, nlat + 1),
        in_specs=specs(cf) + specs(cb),
        out_specs=[pl.BlockSpec((1, L, HP * M_DV), lambda b, h, j: (b, jnp.maximum(j - 1, 0), h)),
                   pl.BlockSpec((1, L, HP * M_DV), lambda b, h, j: (b, jnp.minimum(nlat - j, nlat - 1), h))],
        out_shape=[out_sds, out_sds],
        scratch_shapes=state + state,
        compiler_params=_params(("arbitrary", "arbitrary", "arbitrary")),
        name="mlstm_scan",
    )(z_a, z_a, z_a, zg, z_a, z_a, z_a, zg)


def _mlstm_out_kernel(hf_ref, hb_ref, zo_ref, g_ref, o_ref):
    for h in range(M_HEADS):
        sl = slice(h * M_DV, (h + 1) * M_DV)
        hs = hf_ref[0, :, sl].astype(F32) + hb_ref[0, :, sl].astype(F32)
        ms = jnp.mean(hs * hs, axis=-1, keepdims=True)
        hn = hs * lax.rsqrt(ms + NORM_EPS) * g_ref[:, sl]
        o_ref[0, :, sl] = (hn * jax.nn.sigmoid(zo_ref[0, :, sl].astype(F32))).astype(o_ref.dtype)


def _mlstm_out(h_f, h_b, z_a, gain, tm=256):
    B = h_f.shape[0]
    blk = pl.BlockSpec((1, tm, M_V_W), lambda b, i: (b, i, 0))
    return pl.pallas_call(
        _mlstm_out_kernel,
        grid=(B, SEQ // tm),
        in_specs=[blk, blk,
                  pl.BlockSpec((1, tm, M_V_W), lambda b, i: (b, i, (2 * M_QK_W + M_V_W) // M_V_W)),
                  pl.BlockSpec((1, M_V_W), lambda b, i: (0, 0))],
        out_specs=blk,
        out_shape=jax.ShapeDtypeStruct((B, SEQ, M_V_W), BF16),
        compiler_params=_params(("arbitrary", "arbitrary")),
        name="mlstm_out",
    )(h_f, h_b, z_a, gain.reshape(1, M_V_W))


ATTN_KEY_BLOCK = 768


def _attn_kernel(q_ref, kn_ref, kr_ref, v_ref, o_ref):
    q = q_ref[0]
    m = l = acc = None
    for j in range(ROWS_ALL // ATTN_KEY_BLOCK):
        ks = slice(j * ATTN_KEY_BLOCK, (j + 1) * ATTN_KEY_BLOCK)
        kcat = jnp.concatenate([kn_ref[0, ks, :], kr_ref[0, ks, :]], axis=1)
        s = lax.dot_general(q, kcat, (((1,), (1,)), ((), ())), preferred_element_type=F32)
        m_j = jnp.max(s, axis=1, keepdims=True)
        m_new = m_j if j == 0 else jnp.maximum(m, m_j)
        p = jnp.exp(s - m_new)
        l_j = jnp.sum(p, axis=1, keepdims=True)
        pv = jnp.dot(p.astype(BF16), v_ref[0, ks, :], preferred_element_type=F32)
        if j == 0:
            l, acc = l_j, pv
        else:
            alpha = jnp.exp(m - m_new)
            l = alpha * l + l_j
            acc = alpha * acc + pv
        m = m_new
    o_ref[0] = (acc / l).astype(o_ref.dtype)


def _attention(q, kv, krz, tq=1024):
    B = q.shape[0]
    return pl.pallas_call(
        _attn_kernel,
        grid=(B, A_HEADS, SEQ // tq),
        in_specs=[pl.BlockSpec((1, tq, A_QW), lambda b, h, i: (b, i, h)),
                  pl.BlockSpec((1, ROWS_ALL, A_NOPE), lambda b, h, i: (b, 0, 2 * h)),
                  pl.BlockSpec((1, ROWS_ALL, LANES), lambda b, h, i: (b, 0, 0)),
                  pl.BlockSpec((1, ROWS_ALL, A_DV), lambda b, h, i: (b, 0, 2 * h + 1))],
        out_specs=pl.BlockSpec((1, tq, A_DV), lambda b, h, i: (b, i, h)),
        out_shape=jax.ShapeDtypeStruct((B, SEQ, A_HEADS * A_DV), BF16),
        compiler_params=_params(("arbitrary", "arbitrary", "arbitrary")),
        name="latent_attn",
    )(q, kv, krz, kv)


def _pack_bf16_pair(a, b):
    ua = pltpu.bitcast(a.astype(BF16).astype(F32), jnp.uint32)
    ub = pltpu.bitcast(b.astype(BF16).astype(F32), jnp.uint32)
    return (ua >> 16) | (ub & jnp.uint32(0xFFFF0000))


def _unpack_bf16_pair(u):
    a = pltpu.bitcast(u << 16, F32).astype(BF16)
    b = pltpu.bitcast(u & jnp.uint32(0xFFFF0000), F32).astype(BF16)
    return a, b


def _post1_kernel(x_ref, y_ref, gpost_ref, gpre_ref, g1_ref, sc_ref, sh_ref, rwh_ref, rwl_ref, rb_ref,
                  x1_ref, h2_ref, eidx_ref, gates_ref, rank_ref, cnt_ref, carry_ref):
    first = jnp.logical_and(pl.program_id(0) == 0, pl.program_id(1) == 0)

    @pl.when(first)
    def _():
        carry_ref[...] = jnp.zeros_like(carry_ref)

    y = y_ref[0].astype(F32)
    yn = y * lax.rsqrt(jnp.mean(y * y, axis=-1, keepdims=True) + NORM_EPS) * gpost_ref[...]
    x1 = x_ref[0] + g1_ref[0, 0] * yn
    x1_ref[0] = x1
    hn = x1 * lax.rsqrt(jnp.mean(x1 * x1, axis=-1, keepdims=True) + NORM_EPS) * gpre_ref[...]
    h2 = hn * (1.0 + sc_ref[0, 0]) + sh_ref[0, 0]
    half = D_MODEL // 2
    h2_ref[0] = _pack_bf16_pair(h2[:, :half], h2[:, half:])

    h_hi = h2.astype(BF16)
    h_lo = (h2 - h_hi.astype(F32)).astype(BF16)
    logits = (jnp.dot(h_hi, rwh_ref[...], preferred_element_type=F32)
              + jnp.dot(h_lo, rwh_ref[...], preferred_element_type=F32)
              + jnp.dot(h_hi, rwl_ref[...], preferred_element_type=F32)) + rb_ref[...]
    tm = logits.shape[0]
    lane = lax.broadcasted_iota(jnp.int32, (tm, LANES), 1)
    lane_f = lane.astype(F32)
    work = jnp.where(lane < N_EXPERTS, logits, -jnp.inf)
    sel = jnp.zeros((tm, LANES), F32)
    tops, hots = [], []
    for _ in range(TOP_K):
        mx = jnp.max(work, axis=1, keepdims=True)
        idx = jnp.min(jnp.where(work == mx, lane_f, float(LANES)), axis=1, keepdims=True)
        hot = lane_f == idx
        tops.append(mx)
        hots.append(hot)
        sel = jnp.where(hot, 1.0, sel)
        work = jnp.where(hot, -jnp.inf, work)
    es = [jnp.exp(t - tops[0]) for t in tops]
    tot = es[0] + es[1] + es[2] + es[3]

    r_id = lax.broadcasted_iota(jnp.int32, (tm, tm), 0)
    c_id = lax.broadcasted_iota(jnp.int32, (tm, tm), 1)
    strict = jnp.where(c_id < r_id, 1.0, 0.0).astype(BF16)
    rank_all = jnp.dot(strict, sel.astype(BF16), preferred_element_type=F32) + carry_ref[...]
    carry_ref[...] = carry_ref[...] + jnp.sum(sel, axis=0, keepdims=True)
    cnt_ref[...] = carry_ref[...]

    eidx = jnp.zeros((tm, LANES), jnp.int32)
    gates = jnp.zeros((tm, LANES), F32)
    rank = jnp.zeros((tm, LANES), jnp.int32)
    for kk in range(TOP_K):
        e_k = jnp.sum(jnp.where(hots[kk], lane_f, 0.0), axis=1, keepdims=True)
        r_k = jnp.sum(jnp.where(hots[kk], rank_all, 0.0), axis=1, keepdims=True)
        eidx = jnp.where(lane == kk, e_k.astype(jnp.int32), eidx)
        gates = jnp.where(lane == kk, es[kk] / tot, gates)
        rank = jnp.where(lane == kk, r_k.astype(jnp.int32), rank)
    eidx_ref[0] = eidx
    gates_ref[0] = gates
    rank_ref[0] = rank


def _post1(x, y, gpost, gpre, mod6, rw_hi, rw_lo, rb, tm=256):
    B = x.shape[0]
    blk = pl.BlockSpec((1, tm, D_MODEL), lambda b, i: (b, i, 0))
    vec = pl.BlockSpec((1, D_MODEL), lambda b, i: (0, 0))
    lan = pl.BlockSpec((1, tm, LANES), lambda b, i: (b, i, 0))
    rw = pl.BlockSpec((D_MODEL, LANES), lambda b, i: (0, 0))

    def mod_spec(k):
        return pl.BlockSpec((1, 1, 1, D_MODEL), lambda b, i: (k, b, 0, 0))

    lan_i = jax.ShapeDtypeStruct((B, SEQ, LANES), jnp.int32)
    return pl.pallas_call(
        _post1_kernel,
        grid=(B, SEQ // tm),
        in_specs=[blk, blk, vec, vec, mod_spec(2), mod_spec(4), mod_spec(3), rw, rw,
                  pl.BlockSpec((1, LANES), lambda b, i: (0, 0))],
        out_specs=[blk, pl.BlockSpec((1, tm, D_MODEL // 2), lambda b, i: (b, i, 0)), lan, lan, lan,
                   pl.BlockSpec((1, LANES), lambda b, i: (0, 0))],
        out_shape=[jax.ShapeDtypeStruct((B, SEQ, D_MODEL), F32),
                   jax.ShapeDtypeStruct((B, SEQ, D_MODEL // 2), jnp.uint32),
                   lan_i, jax.ShapeDtypeStruct((B, SEQ, LANES), F32), lan_i,
                   jax.ShapeDtypeStruct((1, LANES), F32)],
        scratch_shapes=[pltpu.VMEM((1, LANES), F32)],
        compiler_params=_params(("arbitrary", "arbitrary")),
        name="post_mix",
    )(x, y, gpost.reshape(1, D_MODEL), gpre.reshape(1, D_MODEL), mod6, mod6, mod6, rw_hi, rw_lo, rb)


DMA_ISSUE_UNROLL = 8


def _gather_kernel(idx_ref, tv_ref, src_ref, o_ref, sem):
    i = pl.program_id(0)
    base = i * MOE_TILE

    @pl.when(tv_ref[i] > 0)
    def _():
        def start(r, c):
            pltpu.make_async_copy(src_ref.at[pl.ds(idx_ref[base + r], 1)], o_ref.at[pl.ds(r, 1)], sem).start()
            return c

        lax.fori_loop(0, MOE_TILE, start, 0, unroll=DMA_ISSUE_UNROLL)
        pltpu.make_async_copy(src_ref.at[pl.ds(0, MOE_TILE)], o_ref, sem).wait()

    @pl.when(tv_ref[i] == 0)
    def _():
        o_ref[...] = jnp.zeros_like(o_ref)


def _gather_rows(slot_tok, tile_v, src):
    n = slot_tok.shape[0]
    return pl.pallas_call(
        _gather_kernel,
        grid_spec=pltpu.PrefetchScalarGridSpec(
            num_scalar_prefetch=2,
            grid=(n // MOE_TILE,),
            in_specs=[pl.BlockSpec(memory_space=pl.ANY)],
            out_specs=pl.BlockSpec((MOE_TILE, src.shape[1]), lambda i, idx, tv: (i, 0)),
            scratch_shapes=[pltpu.SemaphoreType.DMA]),
        out_shape=jax.ShapeDtypeStruct((n, src.shape[1]), src.dtype),
        compiler_params=_params(("arbitrary",)),
        name="moe_gather",
    )(slot_tok, tile_v, src)


def _stream_expert_weights(te_ref, fr_ref, ne_ref, lg_ref, copies, cast):
    p = pl.program_id(0)
    i = pl.program_id(1)

    @pl.when(fr_ref[i] == 1)
    def _():
        @pl.when(jnp.logical_and(p == 0, i == 0))
        def _():
            for c in copies(te_ref[i], p):
                c.start()

        for c in copies(te_ref[i], p):
            c.wait()
        cast()
        is_last_group = lg_ref[i] == 1

        @pl.when(jnp.logical_not(jnp.logical_and(is_last_group, p == pl.num_programs(0) - 1)))
        def _():
            for c in copies(ne_ref[i], p + lg_ref[i]):
                c.start()


def _gmm1_kernel(te_ref, tv_ref, fr_ref, ne_ref, lg_ref, x_ref, bg_ref, bu_ref, w_hbm, o_ref,
                 land_ref, wb_ref, sem):
    i = pl.program_id(1)
    tn = o_ref.shape[1]

    def copies(e, p):
        c0 = pl.multiple_of(p * tn, tn)
        return [pltpu.make_async_copy(w_hbm.at[e, :, pl.ds(c0 + off, tn)], land_ref.at[s], sem.at[s])
                for s, off in enumerate((0, D_EXPERT))]

    def cast():
        for s in range(2):
            wb_ref[s] = land_ref[s].astype(BF16)

    _stream_expert_weights(te_ref, fr_ref, ne_ref, lg_ref, copies, cast)

    @pl.when(tv_ref[i] > 0)
    def _():
        lo, hi = _unpack_bf16_pair(x_ref[...])
        half = D_MODEL // 2
        glu = (jnp.dot(lo, wb_ref[0, :half, :], preferred_element_type=F32)
               + jnp.dot(hi, wb_ref[0, half:, :], preferred_element_type=F32) + bg_ref[0])
        lin = (jnp.dot(lo, wb_ref[1, :half, :], preferred_element_type=F32)
               + jnp.dot(hi, wb_ref[1, half:, :], preferred_element_type=F32) + bu_ref[0])
        glu = jnp.minimum(glu, SWIGLU_LIMIT)
        lin = jnp.clip(lin, -SWIGLU_LIMIT, SWIGLU_LIMIT)
        o_ref[...] = (glu * jax.nn.sigmoid(SWIGLU_ALPHA * glu) * (lin + 1.0)).astype(o_ref.dtype)

    @pl.when(tv_ref[i] == 0)
    def _():
        o_ref[...] = jnp.zeros_like(o_ref)


def _gmm1(tiles, xs, w_gu, b_gu, tn=512):
    n_slots = xs.shape[0]
    n_tiles = n_slots // MOE_TILE
    nf = D_EXPERT // tn
    b3 = b_gu.reshape(N_EXPERTS, 1, 2 * D_EXPERT)
    return pl.pallas_call(
        _gmm1_kernel,
        grid_spec=pltpu.PrefetchScalarGridSpec(
            num_scalar_prefetch=5,
            grid=(nf, n_tiles),
            in_specs=[pl.BlockSpec((MOE_TILE, D_MODEL // 2), lambda n, i, te, *_: (i, 0)),
                      pl.BlockSpec((1, 1, tn), lambda n, i, te, *_: (te[i], 0, n)),
                      pl.BlockSpec((1, 1, tn), lambda n, i, te, *_: (te[i], 0, nf + n)),
                      pl.BlockSpec(memory_space=pl.ANY)],
            out_specs=pl.BlockSpec((MOE_TILE, tn), lambda n, i, te, *_: (i, n)),
            scratch_shapes=[pltpu.VMEM((2, D_MODEL, tn), F32), pltpu.VMEM((2, D_MODEL, tn), BF16),
                            pltpu.SemaphoreType.DMA((2,))]),
        out_shape=jax.ShapeDtypeStruct((n_slots, D_EXPERT), BF16),
        compiler_params=_params(("arbitrary", "arbitrary")),
        name="moe_gate_up",
    )(*tiles, xs, b3, b3, w_gu)


def _gmm2_kernel(te_ref, tv_ref, fr_ref, ne_ref, lg_ref, a_ref, b_ref, w_hbm, o_ref, land_ref, wb_ref, sem):
    i = pl.program_id(1)
    tn = wb_ref.shape[1]

    def copies(e, p):
        return [pltpu.make_async_copy(w_hbm.at[e, :, pl.ds(pl.multiple_of(p * tn, tn), tn)], land_ref, sem)]

    def cast():
        wb_ref[...] = land_ref[...].astype(BF16)

    _stream_expert_weights(te_ref, fr_ref, ne_ref, lg_ref, copies, cast)

    @pl.when(tv_ref[i] > 0)
    def _():
        y = jnp.dot(a_ref[...], wb_ref[...], preferred_element_type=F32) + b_ref[0]
        half = y.shape[1] // 2
        o_ref[...] = _pack_bf16_pair(y[:, :half], y[:, half:])

    @pl.when(tv_ref[i] == 0)
    def _():
        o_ref[...] = jnp.zeros_like(o_ref)


def _gmm2(tiles, act, w_down, b_down, tn):
    n_slots = act.shape[0]
    n_tiles = n_slots // MOE_TILE
    nn = D_MODEL // tn
    return pl.pallas_call(
        _gmm2_kernel,
        grid_spec=pltpu.PrefetchScalarGridSpec(
            num_scalar_prefetch=5,
            grid=(nn, n_tiles),
            in_specs=[pl.BlockSpec((MOE_TILE, D_EXPERT), lambda n, i, te, *_: (i, 0)),
                      pl.BlockSpec((1, 1, tn), lambda n, i, te, *_: (te[i], 0, n)),
                      pl.BlockSpec(memory_space=pl.ANY)],
            out_specs=pl.BlockSpec((MOE_TILE, tn // 2), lambda n, i, te, *_: (i, n)),
            scratch_shapes=[pltpu.VMEM((D_EXPERT, tn), F32), pltpu.VMEM((D_EXPERT, tn), BF16),
                            pltpu.SemaphoreType.DMA]),
        out_shape=jax.ShapeDtypeStruct((n_slots, D_MODEL // 2), jnp.uint32),
        compiler_params=_params(("arbitrary", "arbitrary")),
        name="moe_down",
    )(*tiles, act, b_down.reshape(N_EXPERTS, 1, D_MODEL), w_down)


COMBINE_TM = 256
GMM2_TN = 2048


def _combine_kernel(dest_ref, x1_ref, gates_ref, gpost_ref, g2_ref, ys_ref, o_ref, buf_ref, sem):
    tm = COMBINE_TM
    n_rows = tm * TOP_K
    n_steps = pl.num_programs(0) * pl.num_programs(1)
    step = pl.program_id(0) * pl.num_programs(1) + pl.program_id(1)
    slot = step % 2

    def issue(st, sl):
        base = st * n_rows

        def start(r, c):
            pltpu.make_async_copy(ys_ref.at[pl.ds(dest_ref[base + r], 1)],
                                  buf_ref.at[sl, pl.ds(r, 1)], sem.at[sl]).start()
            return c

        lax.fori_loop(0, n_rows, start, 0, unroll=DMA_ISSUE_UNROLL)

    @pl.when(step == 0)
    def _():
        issue(0, 0)

    @pl.when(step + 1 < n_steps)
    def _():
        issue(step + 1, 1 - slot)

    pltpu.make_async_copy(ys_ref.at[pl.ds(0, n_rows)], buf_ref.at[slot], sem.at[slot]).wait()

    gates = gates_ref[0]
    lane = lax.broadcasted_iota(jnp.int32, gates.shape, 1)
    half = GMM2_TN // 2
    parts = [None] * (2 * (D_MODEL // GMM2_TN))
    for kk in range(TOP_K):
        gk = jnp.sum(jnp.where(lane == kk, gates, 0.0), axis=1, keepdims=True)
        rows = buf_ref[slot, pl.ds(kk * tm, tm), :]
        for n in range(D_MODEL // GMM2_TN):
            lo, hi = _unpack_bf16_pair(rows[:, n * half:(n + 1) * half])
            for t, val in ((2 * n, lo), (2 * n + 1, hi)):
                term = gk * val.astype(F32)
                parts[t] = term if parts[t] is None else parts[t] + term
    f = jnp.concatenate(parts, axis=1)
    fn = f * lax.rsqrt(jnp.mean(f * f, axis=-1, keepdims=True) + NORM_EPS) * gpost_ref[...]
    o_ref[0] = x1_ref[0] + g2_ref[0, 0] * fn


def _combine(dest, x1, gates, gpost, mod6, ys):
    B = x1.shape[0]
    tm = COMBINE_TM
    blk = pl.BlockSpec((1, tm, D_MODEL), lambda b, i, d: (b, i, 0))
    return pl.pallas_call(
        _combine_kernel,
        grid_spec=pltpu.PrefetchScalarGridSpec(
            num_scalar_prefetch=1,
            grid=(B, SEQ // tm),
            in_specs=[blk,
                      pl.BlockSpec((1, tm, LANES), lambda b, i, d: (b, i, 0)),
                      pl.BlockSpec((1, D_MODEL), lambda b, i, d: (0, 0)),
                      pl.BlockSpec((1, 1, 1, D_MODEL), lambda b, i, d: (5, b, 0, 0)),
                      pl.BlockSpec(memory_space=pl.ANY)],
            out_specs=blk,
            scratch_shapes=[pltpu.VMEM((2, tm * TOP_K, D_MODEL // 2), jnp.uint32),
                            pltpu.SemaphoreType.DMA((2,))]),
        out_shape=jax.ShapeDtypeStruct((B, SEQ, D_MODEL), F32),
        compiler_params=_params(("arbitrary", "arbitrary")),
        name="moe_combine",
    )(dest, x1, gates, gpost.reshape(1, D_MODEL), mod6, ys)


def _rope_tables():
    rows = SEQ // GRID_W
    r, col = jnp.meshgrid(jnp.arange(rows, dtype=F32), jnp.arange(GRID_W, dtype=F32), indexing="ij")
    n_freq = A_ROPE // 4
    inv = ROPE_BASE ** (-jnp.arange(n_freq, dtype=F32) / n_freq)
    ang = jnp.concatenate([r.reshape(-1, 1) * inv, col.reshape(-1, 1) * inv], axis=-1)
    cos = jnp.concatenate([jnp.repeat(jnp.cos(ang), 2, axis=-1), jnp.ones((CTX_LEN, A_ROPE), F32)], axis=0)
    sin = jnp.concatenate([jnp.repeat(jnp.sin(ang), 2, axis=-1), jnp.zeros((CTX_LEN, A_ROPE), F32)], axis=0)
    pad = ((0, 0), (0, LANES - A_ROPE))
    return jnp.concatenate([cos, sin], axis=-1)[None], jnp.pad(cos, pad)[None], jnp.pad(sin, pad)[None]


def _rot_partner(w):
    wp = w.reshape(*w.shape[:-1], A_ROPE // 2, 2)
    return jnp.stack([-wp[..., 1], wp[..., 0]], axis=-1).reshape(w.shape)


def _layer(x, ctx, mod6, tabs, norm_pre_mix, norm_post_mix, norm_pre_ffn, norm_post_ffn, w_in, b_gates,
           m_out_norm, q_norm, kv_norm, w_uq, w_ukv, w_branch_a, w_branch_b, w_out, router_w, router_b,
           w_gu, b_gu, w_down, b_down):
    B = x.shape[0]

    tab, cos_k, sin_k = tabs
    w_in_t = w_in.T
    gate_bias = jnp.concatenate([b_gates, jnp.zeros((LANES - N_GATE_COLS,), F32)]).reshape(1, LANES)
    wq = w_uq.reshape(Q_LORA, A_HEADS, A_NOPE + A_ROPE)
    wq_r = wq[..., A_NOPE:]
    w_uq2 = jnp.concatenate([wq[..., :A_NOPE], wq_r, _rot_partner(wq_r)], axis=-1)
    w_uq2 = w_uq2.reshape(Q_LORA, A_HEADS * A_QW).astype(BF16)

    h_all = _prenorm(x, ctx, norm_pre_mix, mod6)

    z_a = _matmul(h_all, w_in_t, rows=ROWS_ALL, tm=1152, tn=512, n_out=OFF_G, out_dtype=BF16, wt_row0=OFF_Q,
                  name="in_proj_a")
    z_b = _matmul(h_all, w_in_t, rows=ROWS_ALL, tm=1152, tn=512, n_out=Q_LORA + KV_LORA, out_dtype=BF16,
                  wt_row0=OFF_CQ, name="in_proj_b")
    z_d = _matmul(h_all, w_in_t, rows=SEQ, tm=1024, tn=512, n_out=2 * D_MODEL, out_dtype=BF16, wt_row0=OFF_GA,
                  name="in_proj_d")
    zg, krz = _small_proj(h_all, w_in_t, gate_bias, cos_k, sin_k, tm=1152)

    h_f, h_b = _mlstm(z_a, zg)
    ya = _mlstm_out(h_f, h_b, z_a, m_out_norm)

    q = _matmul(z_b, w_uq2, rows=SEQ, tm=1024, tn=1024, n_out=A_HEADS * A_QW, out_dtype=BF16, k=Q_LORA,
                rms_gain=q_norm, rope_tab=tab, scale=A_SCALE, name="q_up")
    kv = _matmul(z_b, w_ukv, rows=ROWS_ALL, tm=1152, tn=1024, n_out=A_HEADS * (A_NOPE + A_DV), out_dtype=BF16,
                 k=KV_LORA, x_col=Q_LORA // KV_LORA, rms_gain=kv_norm, name="kv_up")
    yb = _attention(q, kv, krz)

    ua = _matmul(ya, w_branch_a, rows=SEQ, tm=1024, tn=512, n_out=D_MODEL, out_dtype=BF16, gate=z_d,
                 name="branch_a")
    u = _matmul(yb, w_branch_b, rows=SEQ, tm=1024, tn=512, n_out=D_MODEL, out_dtype=BF16, gate=z_d,
                gate_col=D_MODEL // 512, add=ua, name="branch_b")
    y = _matmul(u, w_out, rows=SEQ, tm=1024, tn=512, n_out=D_MODEL, out_dtype=BF16, name="out_proj")

    rw = jnp.concatenate([router_w, jnp.zeros((D_MODEL, LANES - N_EXPERTS), F32)], axis=1)
    rw_hi = rw.astype(BF16)
    rw_lo = (rw - rw_hi.astype(F32)).astype(BF16)
    rb = jnp.concatenate([router_b, jnp.zeros((LANES - N_EXPERTS,), F32)]).reshape(1, LANES)
    x1, h2p, eidx, gates, rank, counts = _post1(x, y, norm_post_mix, norm_pre_ffn, mod6, rw_hi, rw_lo, rb)

    T = B * SEQ
    n_tiles = -(-(T * TOP_K + N_EXPERTS * (MOE_TILE - 1)) // MOE_TILE)
    n_slots = n_tiles * MOE_TILE
    cnt = counts[0, :N_EXPERTS].astype(jnp.int32)
    padded = (cnt + MOE_TILE - 1) // MOE_TILE * MOE_TILE
    g_end = jnp.cumsum(padded)
    g_start = g_end - padded
    e_flat = eidx.reshape(T, LANES)[:, :TOP_K]
    dest = g_start[e_flat] + rank.reshape(T, LANES)[:, :TOP_K]
    tok = jnp.broadcast_to(jnp.arange(T, dtype=jnp.int32)[:, None], (T, TOP_K))
    slot_tok = jnp.zeros((n_slots,), jnp.int32).at[dest.reshape(-1)].set(tok.reshape(-1))
    tile_lo = jnp.arange(n_tiles, dtype=jnp.int32) * MOE_TILE
    tile_e = jnp.minimum(jnp.sum((g_end[None, :] <= tile_lo[:, None]).astype(jnp.int32), axis=1), N_EXPERTS - 1)
    tile_v = (tile_lo < g_end[-1]).astype(jnp.int32)
    prev_e = jnp.concatenate([jnp.full((1,), -1, jnp.int32), tile_e[:-1]])
    tile_first = (tile_v * (tile_e != prev_e)).astype(jnp.int32)
    after = g_end[tile_e] // MOE_TILE
    has_after = after * MOE_TILE < g_end[-1]
    tile_next_e = jnp.where(has_after, tile_e[jnp.minimum(after, n_tiles - 1)], tile_e[0]).astype(jnp.int32)
    tile_last = jnp.logical_not(has_after).astype(jnp.int32)
    tiles = (tile_e, tile_v, tile_first, tile_next_e, tile_last)
    dest_t = dest.reshape(T // COMBINE_TM, COMBINE_TM, TOP_K).transpose(0, 2, 1).reshape(-1)

    xs = _gather_rows(slot_tok, tile_v, h2p.reshape(T, D_MODEL // 2))
    act = _gmm1(tiles, xs, w_gu, b_gu)
    ys = _gmm2(tiles, act, w_down, b_down, tn=GMM2_TN)
    return _combine(dest_t, x1, gates, norm_post_ffn, mod6, ys)


def kernel(x, c, ctx, c_ctx, w_ada, b_ada, norm_pre_mix, norm_post_mix, norm_pre_ffn, norm_post_ffn, w_in,
           b_gates, m_out_norm, q_norm, kv_norm, w_uq, w_ukv, w_branch_a, w_branch_b, w_out, router_w, router_b,
           w_gu, b_gu, w_down, b_down):
    B = x.shape[0]
    depth = w_ada.shape[0]
    assert depth == 1, "context-stream outputs between layers are not implemented"
    tabs = _rope_tables()
    cvec = jnp.concatenate([c, c_ctx[None], jnp.zeros((8 - B - 1, D_MODEL), F32)], axis=0)
    l = 0
    mod = _ada(cvec, w_ada[l], b_ada[l])
    mod6 = mod.reshape(8, N_MOD, 1, D_MODEL).transpose(1, 0, 2, 3)
    return _layer(x, ctx, mod6, tabs, norm_pre_mix[l], norm_post_mix[l], norm_pre_ffn[l], norm_post_ffn[l],
                  w_in[l], b_gates[l], m_out_norm[l], q_norm[l], kv_norm[l], w_uq[l], w_ukv[l], w_branch_a[l],
                  w_branch_b[l], w_out[l], router_w[l], router_b[l], w_gu[l], b_gu[l], w_down[l], b_down[l])
```

```python
import functools

import jax
import jax.numpy as jnp
import numpy as np
from jax import lax
from jax.experimental import pallas as pl
from jax.experimental.pallas import tpu as pltpu

F32 = jnp.float32
BF16 = jnp.bfloat16

D_MODEL = 4096
SEQ = 2048
GRID_W = 64
CTX_LEN = 256
ROWS_ALL = SEQ + CTX_LEN
NORM_EPS = 1e-6
N_MOD = 6

M_HEADS = 8
M_DQK = 256
M_DV = 512
M_QK_W = M_HEADS * M_DQK
M_V_W = M_HEADS * M_DV
N_GATE_COLS = 4 * M_HEADS
M_CHUNK = 256

A_HEADS = 32
A_NOPE = 128
A_ROPE = 64
A_DV = 128
Q_LORA = 1024
KV_LORA = 512
ROPE_BASE = 10000.0
A_SCALE = (A_NOPE + A_ROPE) ** -0.5
A_QW = 256

N_EXPERTS = 32
TOP_K = 4
D_EXPERT = 1536
SWIGLU_LIMIT = 7.0
SWIGLU_ALPHA = 1.702
MOE_TILE = 384

OFF_Q = 0
OFF_G = 2 * M_QK_W + 2 * M_V_W
OFF_CQ = OFF_G + N_GATE_COLS
OFF_CKV = OFF_CQ + Q_LORA
OFF_KR = OFF_CKV + KV_LORA
OFF_GA = OFF_KR + A_ROPE
OFF_GB = OFF_GA + D_MODEL

LANES = 128
V7X_VMEM_LIMIT = 56 * 1024 * 1024


def _params(sem, vmem=V7X_VMEM_LIMIT):
    return pltpu.CompilerParams(dimension_semantics=sem, vmem_limit_bytes=vmem)


def _split3(x):
    hi = x.astype(BF16)
    r1 = x - hi.astype(F32)
    mid = r1.astype(BF16)
    lo = (r1 - mid.astype(F32)).astype(BF16)
    return hi, mid, lo


def _ada_kernel(c_ref, w_ref, b_ref, o_ref):
    c = c_ref[...]
    cs = c * jax.nn.sigmoid(c)
    c_hi, c_mid, _ = _split3(cs)
    w = w_ref[...]
    w_hi = w.astype(BF16)
    w_lo = (w - w_hi.astype(F32)).astype(BF16)
    acc = jnp.dot(c_hi, w_hi, preferred_element_type=F32)
    acc += jnp.dot(c_mid, w_hi, preferred_element_type=F32)
    acc += jnp.dot(c_hi, w_lo, preferred_element_type=F32)
    o_ref[...] = acc + b_ref[...]


def _ada(cvec, w_ada, b_ada):
    n = w_ada.shape[1]
    tn = 1024
    return pl.pallas_call(
        _ada_kernel,
        grid=(n // tn,),
        in_specs=[pl.BlockSpec((8, D_MODEL), lambda j: (0, 0)),
                  pl.BlockSpec((D_MODEL, tn), lambda j: (0, j)),
                  pl.BlockSpec((1, tn), lambda j: (0, j))],
        out_specs=pl.BlockSpec((8, tn), lambda j: (0, j)),
        out_shape=jax.ShapeDtypeStruct((8, n), F32),
        compiler_params=_params(("arbitrary",)),
        name="ada_mod",
    )(cvec, w_ada, b_ada.reshape(1, n))


def _prenorm_kernel(x_ref, c_ref, g_ref, sc_ref, sh_ref, o_ref):
    j = pl.program_id(1)
    nlat = SEQ // CTX_LEN

    def run(src):
        xf = src[0]
        ms = jnp.mean(xf * xf, axis=-1, keepdims=True)
        y = xf * lax.rsqrt(ms + NORM_EPS) * g_ref[...]
        o_ref[0] = (y * (1.0 + sc_ref[0, 0]) + sh_ref[0, 0]).astype(o_ref.dtype)

    @pl.when(j < nlat)
    def _():
        run(x_ref)

    @pl.when(j == nlat)
    def _():
        run(c_ref)


def _prenorm(x, ctx, gain, mod6):
    B = x.shape[0]
    nlat = SEQ // CTX_LEN
    blk = (1, CTX_LEN, D_MODEL)

    def mod_spec(k):
        return pl.BlockSpec((1, 1, 1, D_MODEL),
                            lambda b, j: (k, jnp.where(j == nlat, B, b), 0, 0))

    return pl.pallas_call(
        _prenorm_kernel,
        grid=(B, nlat + 1),
        in_specs=[pl.BlockSpec(blk, lambda b, j: (b, jnp.minimum(j, nlat - 1), 0)),
                  pl.BlockSpec(blk, lambda b, j: (b, 0, 0)),
                  pl.BlockSpec((1, D_MODEL), lambda b, j: (0, 0)),
                  mod_spec(1), mod_spec(0)],
        out_specs=pl.BlockSpec(blk, lambda b, j: (b, j, 0)),
        out_shape=jax.ShapeDtypeStruct((B, ROWS_ALL, D_MODEL), BF16),
        compiler_params=_params(("arbitrary", "arbitrary")),
        name="prenorm",
    )(x, ctx, gain.reshape(1, D_MODEL), mod6, mod6)


def _mm_kernel(*refs, rms, gate, add, rope, scale, w_transposed):
    refs = list(refs)
    x_ref = refs.pop(0)
    w_ref = refs.pop(0)
    gain_ref = refs.pop(0) if rms else None
    gate_ref = refs.pop(0) if gate else None
    add_ref = refs.pop(0) if add else None
    tab_ref = refs.pop(0) if rope else None
    o_ref = refs.pop(0)
    xn_ref = refs.pop(0) if rms else None

    if rms:
        @pl.when(pl.program_id(2) == 0)
        def _():
            xf = x_ref[0].astype(F32)
            ms = jnp.mean(xf * xf, axis=-1, keepdims=True)
            xn_ref[...] = (xf * lax.rsqrt(ms + NORM_EPS) * gain_ref[...]).astype(BF16)
        xv = xn_ref[...]
    else:
        xv = x_ref[0]
    contract = (((1,), (1,)), ((), ())) if w_transposed else (((1,), (0,)), ((), ()))
    acc = lax.dot_general(xv, w_ref[...].astype(BF16), contract, preferred_element_type=F32)
    if gate:
        acc = jax.nn.sigmoid(gate_ref[0].astype(F32)) * acc
    if add:
        acc = acc + add_ref[0].astype(F32)
    if rope:
        tab = tab_ref[0]
        for hh in range(acc.shape[1] // A_QW):
            c0 = hh * A_QW
            r = acc[:, c0 + LANES:c0 + A_QW] * tab
            r = r + pltpu.roll(r, A_ROPE, axis=1)
            o_ref[0, :, c0:c0 + LANES] = (acc[:, c0:c0 + LANES] * scale).astype(o_ref.dtype)
            o_ref[0, :, c0 + LANES:c0 + A_QW] = (r * scale).astype(o_ref.dtype)
    else:
        o_ref[0] = acc.astype(o_ref.dtype)


def _matmul(x, w, *, rows, tm, tn, n_out, out_dtype, x_col=0, k=None, w_col=0, wt_row0=None,
            rms_gain=None, gate=None, gate_col=0, add=None, rope_tab=None, scale=1.0, name="mm"):
    B = x.shape[0]
    k = x.shape[2] if k is None else k
    assert rows % tm == 0 and n_out % tn == 0
    if wt_row0 is None:
        assert w.shape[0] == k
        w_spec = pl.BlockSpec((k, tn), lambda b, i, j: (0, j + w_col))
    else:
        assert w.shape[1] == k and wt_row0 % 8 == 0
        w_spec = pl.BlockSpec((pl.Element(tn), pl.Element(k)),
                              lambda b, i, j: (pl.multiple_of(wt_row0 + j * tn, 8), 0))
    in_specs = [pl.BlockSpec((1, tm, k), lambda b, i, j: (b, i, x_col)), w_spec]
    args = [x, w]
    scratch = []
    if rms_gain is not None:
        in_specs.append(pl.BlockSpec((1, k), lambda b, i, j: (0, 0)))
        args.append(rms_gain.reshape(1, k))
        scratch.append(pltpu.VMEM((tm, k), BF16))
    if gate is not None:
        in_specs.append(pl.BlockSpec((1, tm, tn), lambda b, i, j: (b, i, j + gate_col)))
        args.append(gate)
    if add is not None:
        in_specs.append(pl.BlockSpec((1, tm, tn), lambda b, i, j: (b, i, j)))
        args.append(add)
    if rope_tab is not None:
        in_specs.append(pl.BlockSpec((1, tm, LANES), lambda b, i, j: (0, i, 0)))
        args.append(rope_tab)
    kern = functools.partial(_mm_kernel, rms=rms_gain is not None, gate=gate is not None, add=add is not None,
                             rope=rope_tab is not None, scale=scale, w_transposed=wt_row0 is not None)
    return pl.pallas_call(
        kern,
        grid=(B, rows // tm, n_out // tn),
        in_specs=in_specs,
        out_specs=pl.BlockSpec((1, tm, tn), lambda b, i, j: (b, i, j)),
        out_shape=jax.ShapeDtypeStruct((B, rows, n_out), out_dtype),
        scratch_shapes=scratch,
        compiler_params=_params(("arbitrary", "arbitrary", "arbitrary")),
        name=name,
    )(*args)


def _small_kernel(x_ref, wg_ref, wk_ref, bias_ref, cos_ref, sin_ref, zg_ref, kr_ref):
    xv = x_ref[0]

    def proj(w_ref):
        w = w_ref[...]
        w_hi = w.astype(BF16)
        w_lo = (w - w_hi.astype(F32)).astype(BF16)
        nt = (((1,), (1,)), ((), ()))
        return (lax.dot_general(xv, w_hi, nt, preferred_element_type=F32)
                + lax.dot_general(xv, w_lo, nt, preferred_element_type=F32))

    zg_ref[0] = proj(wg_ref) + bias_ref[...]
    kr = proj(wk_ref)
    lane = lax.broadcasted_iota(jnp.int32, kr.shape, 1)
    partner = jnp.where((lane & 1) == 0, -pltpu.roll(kr, LANES - 1, axis=1), pltpu.roll(kr, 1, axis=1))
    kr_ref[0] = (kr * cos_ref[0] + partner * sin_ref[0]).astype(kr_ref.dtype)


def _small_proj(h_all, w_in_t, bias, cos_k, sin_k, tm):
    B = h_all.shape[0]
    blk = lambda c: pl.BlockSpec((1, tm, c), lambda b, i: (b, i, 0))
    tab = pl.BlockSpec((1, tm, LANES), lambda b, i: (0, i, 0))
    w_rows = lambda r0: pl.BlockSpec((pl.Element(LANES), pl.Element(D_MODEL)), lambda b, i: (r0, 0))
    return pl.pallas_call(
        _small_kernel,
        grid=(B, ROWS_ALL // tm),
        in_specs=[blk(D_MODEL), w_rows(OFF_G), w_rows(OFF_KR),
                  pl.BlockSpec((1, LANES), lambda b, i: (0, 0)),
                  tab, tab],
        out_specs=[blk(LANES), blk(LANES)],
        out_shape=[jax.ShapeDtypeStruct((B, ROWS_ALL, LANES), F32),
                   jax.ShapeDtypeStruct((B, ROWS_ALL, LANES), BF16)],
        compiler_params=_params(("arbitrary", "arbitrary")),
        name="small_proj",
    )(h_all, w_in_t, w_in_t, bias, cos_k, sin_k)


def _log_sigmoid(x):
    return jnp.minimum(x, 0.0) - jnp.log1p(jnp.exp(-jnp.abs(x)))


MLSTM_HP = 4


def _mlstm_dir(q_ref, k_ref, v_ref, zg_ref, o_ref, c_ref, n_ref, m_ref, *, reverse, head0):
    L = M_CHUNK
    zg = zg_ref[0]
    lane = lax.broadcasted_iota(jnp.int32, (L, LANES), 1)
    sub = lax.broadcasted_iota(jnp.int32, (LANES, L), 0)
    row_id = lax.broadcasted_iota(jnp.int32, (L, L), 0)
    col_id = lax.broadcasted_iota(jnp.int32, (L, L), 1)
    seen = (col_id >= row_id) if reverse else (col_id <= row_id)

    lf = _log_sigmoid(zg)
    tri = jnp.where(seen, 1.0, 0.0).astype(BF16)
    hi, mid, lo = _split3(lf)
    bcum = (jnp.dot(tri, hi, preferred_element_type=F32)
            + jnp.dot(tri, mid, preferred_element_type=F32)
            + jnp.dot(tri, lo, preferred_element_type=F32))

    bcum_t = bcum.T
    zg_t = zg.T

    def pick_col(a, c):
        return jnp.sum(jnp.where(lane == c, a, 0.0), axis=1, keepdims=True)

    def pick_row(a_t, c):
        return jnp.sum(jnp.where(sub == c, a_t, 0.0), axis=0, keepdims=True)

    for hh in range(MLSTM_HP):
        col_i = head0 + hh + (2 * M_HEADS if reverse else 0)
        col_f = col_i + M_HEADS
        q = q_ref[0, :, hh * M_DQK:(hh + 1) * M_DQK]
        k = k_ref[0, :, hh * M_DQK:(hh + 1) * M_DQK]
        v = v_ref[0, :, hh * M_DV:(hh + 1) * M_DV]
        c_prev = c_ref[hh]
        n_prev = n_ref[hh]
        m_prev = m_ref[hh]

        b_col = pick_col(bcum, col_f)
        i_col = pick_col(zg, col_i)
        b_row = pick_row(bcum_t, col_f)
        i_row = pick_row(zg_t, col_i)
        b_last = b_col[0:1, :] if reverse else b_col[L - 1:L, :]

        d = jnp.where(seen, b_col - b_row + i_row, -jnp.inf)
        inter = b_col + m_prev
        m_t = jnp.maximum(inter, jnp.max(d, axis=1, keepdims=True))
        w = jnp.exp(d - m_t) * (M_DQK ** -0.5)
        scale = jnp.exp(inter - m_t)
        s = lax.dot_general(q, k, (((1,), (1,)), ((), ())), preferred_element_type=F32) * w
        cq = lax.dot_general(q, c_prev.astype(BF16), (((1,), (1,)), ((), ())),
                             preferred_element_type=F32)
        num = jnp.dot(s.astype(BF16), v, preferred_element_type=F32) + scale * cq
        nq = jnp.sum(q.astype(F32) * n_prev, axis=1, keepdims=True)
        den = jnp.sum(s, axis=1, keepdims=True) + scale * nq
        h_out = num / jnp.maximum(jnp.abs(den), jnp.exp(-m_t))
        o_ref[0, :, hh * M_DV:(hh + 1) * M_DV] = h_out.astype(o_ref.dtype)

        g = b_last - b_col + i_col
        m_new = jnp.maximum(b_last + m_prev, jnp.max(g, axis=0, keepdims=True))
        wk = jnp.exp(g - m_new) * (M_DQK ** -0.5)
        dec = jnp.exp(b_last + m_prev - m_new)
        vw = (v.astype(F32) * wk).astype(BF16)
        upd = lax.dot_general(vw, k, (((0,), (0,)), ((), ())), preferred_element_type=F32)
        c_ref[hh] = dec * c_prev + upd
        n_ref[hh] = dec * n_prev + jnp.sum(k.astype(F32) * wk, axis=0, keepdims=True)
        m_ref[hh] = m_new


def _mlstm_kernel(qf, kf, vf, gf, qb, kb, vb, gb, of, ob, cf, nf, mf, cb, nb, mb):
    head0 = pl.program_id(1) * MLSTM_HP

    @pl.when(pl.program_id(2) == 0)
    def _():
        for r in (cf, nf, mf, cb, nb, mb):
            r[...] = jnp.zeros_like(r)

    _mlstm_dir(qf, kf, vf, gf, of, cf, nf, mf, reverse=False, head0=head0)
    _mlstm_dir(qb, kb, vb, gb, ob, cb, nb, mb, reverse=True, head0=head0)


def _mlstm(z_a, zg):
    B = z_a.shape[0]
    L = M_CHUNK
    HP = MLSTM_HP
    nlat = SEQ // L
    assert CTX_LEN == L and M_HEADS % HP == 0
    k_blk0 = M_QK_W // (HP * M_DQK)
    v_blk0 = (2 * M_QK_W) // (HP * M_DV)

    def cf(j):
        return jnp.where(j == 0, nlat, j - 1)

    def cb(j):
        return jnp.where(j == 0, nlat, nlat - j)

    def specs(ch):
        return [pl.BlockSpec((1, L, HP * M_DQK), lambda b, h, j: (b, ch(j), h)),
                pl.BlockSpec((1, L, HP * M_DQK), lambda b, h, j: (b, ch(j), k_blk0 + h)),
                pl.BlockSpec((1, L, HP * M_DV), lambda b, h, j: (b, ch(j), v_blk0 + h)),
                pl.BlockSpec((1, L, LANES), lambda b, h, j: (b, ch(j), 0))]

    out_sds = jax.ShapeDtypeStruct((B, SEQ, M_V_W), BF16)
    state = [pltpu.VMEM((HP, M_DV, M_DQK), F32), pltpu.VMEM((HP, 1, M_DQK), F32),
             pltpu.VMEM((HP, 1, 1), F32)]
    return pl.pallas_call(
        _mlstm_kernel,
        grid=(B, M_HEADS // HP, nlat + 1),
        in_specs=specs(cf) + specs(cb),
        out_specs=[pl.BlockSpec((1, L, HP * M_DV), lambda b, h, j: (b, jnp.maximum(j - 1, 0), h)),
                   pl.BlockSpec((1, L, HP * M_DV), lambda b, h, j: (b, jnp.minimum(nlat - j, nlat - 1), h))],
        out_shape=[out_sds, out_sds],
        scratch_shapes=state + state,
        compiler_params=_params(("arbitrary", "arbitrary", "arbitrary")),
        name="mlstm_scan",
    )(z_a, z_a, z_a, zg, z_a, z_a, z_a, zg)


def _mlstm_out_kernel(hf_ref, hb_ref, zo_ref, g_ref, o_ref):
    for h in range(M_HEADS):
        sl = slice(h * M_DV, (h + 1) * M_DV)
        hs = hf_ref[0, :, sl].astype(F32) + hb_ref[0, :, sl].astype(F32)
        ms = jnp.mean(hs * hs, axis=-1, keepdims=True)
        hn = hs * lax.rsqrt(ms + NORM_EPS) * g_ref[:, sl]
        o_ref[0, :, sl] = (hn * jax.nn.sigmoid(zo_ref[0, :, sl].astype(F32))).astype(o_ref.dtype)


def _mlstm_out(h_f, h_b, z_a, gain, tm=256):
    B = h_f.shape[0]
    blk = pl.BlockSpec((1, tm, M_V_W), lambda b, i: (b, i, 0))
    return pl.pallas_call(
        _mlstm_out_kernel,
        grid=(B, SEQ // tm),
        in_specs=[blk, blk,
                  pl.BlockSpec((1, tm, M_V_W), lambda b, i: (b, i, (2 * M_QK_W + M_V_W) // M_V_W)),
                  pl.BlockSpec((1, M_V_W), lambda b, i: (0, 0))],
        out_specs=blk,
        out_shape=jax.ShapeDtypeStruct((B, SEQ, M_V_W), BF16),
        compiler_params=_params(("arbitrary", "arbitrary")),
        name="mlstm_out",
    )(h_f, h_b, z_a, gain.reshape(1, M_V_W))


ATTN_KEY_BLOCK = 768


ATTN_HP = 2


def _attn_kernel(q_ref, kv_ref, kr_ref, o_ref):
    kvw = A_NOPE + A_DV
    m = [None] * ATTN_HP
    l = [None] * ATTN_HP
    acc = [None] * ATTN_HP
    for j in range(ROWS_ALL // ATTN_KEY_BLOCK):
        ks = slice(j * ATTN_KEY_BLOCK, (j + 1) * ATTN_KEY_BLOCK)
        kr = kr_ref[0, ks, :]
        for hh in range(ATTN_HP):
            q = q_ref[0, :, hh * A_QW:(hh + 1) * A_QW]
            kcat = jnp.concatenate([kv_ref[0, ks, hh * kvw:hh * kvw + A_NOPE], kr], axis=1)
            s = lax.dot_general(q, kcat, (((1,), (1,)), ((), ())), preferred_element_type=F32)
            m_j = jnp.max(s, axis=1, keepdims=True)
            m_new = m_j if j == 0 else jnp.maximum(m[hh], m_j)
            p = jnp.exp(s - m_new)
            l_j = jnp.sum(p, axis=1, keepdims=True)
            pv = jnp.dot(p.astype(BF16), kv_ref[0, ks, hh * kvw + A_NOPE:(hh + 1) * kvw],
                         preferred_element_type=F32)
            if j == 0:
                l[hh], acc[hh] = l_j, pv
            else:
                alpha = jnp.exp(m[hh] - m_new)
                l[hh] = alpha * l[hh] + l_j
                acc[hh] = alpha * acc[hh] + pv
            m[hh] = m_new
    for hh in range(ATTN_HP):
        o_ref[0, :, hh * A_DV:(hh + 1) * A_DV] = (acc[hh] / l[hh]).astype(o_ref.dtype)


def _attention(q, kv, krz, tq=1024):
    B = q.shape[0]
    HP = ATTN_HP
    return pl.pallas_call(
        _attn_kernel,
        grid=(B, A_HEADS // HP, SEQ // tq),
        in_specs=[pl.BlockSpec((1, tq, HP * A_QW), lambda b, h, i: (b, i, h)),
                  pl.BlockSpec((1, ROWS_ALL, HP * (A_NOPE + A_DV)), lambda b, h, i: (b, 0, h)),
                  pl.BlockSpec((1, ROWS_ALL, LANES), lambda b, h, i: (b, 0, 0))],
        out_specs=pl.BlockSpec((1, tq, HP * A_DV), lambda b, h, i: (b, i, h)),
        out_shape=jax.ShapeDtypeStruct((B, SEQ, A_HEADS * A_DV), BF16),
        compiler_params=_params(("arbitrary", "arbitrary", "arbitrary")),
        name="latent_attn",
    )(q, kv, krz)


def _pack_bf16_pair(a, b):
    ua = pltpu.bitcast(a.astype(BF16).astype(F32), jnp.uint32)
    ub = pltpu.bitcast(b.astype(BF16).astype(F32), jnp.uint32)
    return (ua >> 16) | (ub & jnp.uint32(0xFFFF0000))


def _unpack_bf16_pair(u):
    a = pltpu.bitcast(u << 16, F32).astype(BF16)
    b = pltpu.bitcast(u & jnp.uint32(0xFFFF0000), F32).astype(BF16)
    return a, b


def _post1_kernel(x_ref, y_ref, gpost_ref, gpre_ref, g1_ref, sc_ref, sh_ref, rwh_ref, rwl_ref, rb_ref,
                  x1_ref, h2_ref, eidx_ref, gates_ref, rank_ref, cnt_ref, carry_ref):
    first = jnp.logical_and(pl.program_id(0) == 0, pl.program_id(1) == 0)

    @pl.when(first)
    def _():
        carry_ref[...] = jnp.zeros_like(carry_ref)

    y = y_ref[0].astype(F32)
    yn = y * lax.rsqrt(jnp.mean(y * y, axis=-1, keepdims=True) + NORM_EPS) * gpost_ref[...]
    x1 = x_ref[0] + g1_ref[0, 0] * yn
    x1_ref[0] = x1
    hn = x1 * lax.rsqrt(jnp.mean(x1 * x1, axis=-1, keepdims=True) + NORM_EPS) * gpre_ref[...]
    h2 = hn * (1.0 + sc_ref[0, 0]) + sh_ref[0, 0]
    half = D_MODEL // 2
    h2_ref[0] = _pack_bf16_pair(h2[:, :half], h2[:, half:])

    h_hi = h2.astype(BF16)
    h_lo = (h2 - h_hi.astype(F32)).astype(BF16)
    logits = (jnp.dot(h_hi, rwh_ref[...], preferred_element_type=F32)
              + jnp.dot(h_lo, rwh_ref[...], preferred_element_type=F32)
              + jnp.dot(h_hi, rwl_ref[...], preferred_element_type=F32)) + rb_ref[...]
    tm = logits.shape[0]
    lane = lax.broadcasted_iota(jnp.int32, (tm, LANES), 1)
    lane_f = lane.astype(F32)
    work = jnp.where(lane < N_EXPERTS, logits, -jnp.inf)
    sel = jnp.zeros((tm, LANES), F32)
    tops, hots = [], []
    for _ in range(TOP_K):
        mx = jnp.max(work, axis=1, keepdims=True)
        idx = jnp.min(jnp.where(work == mx, lane_f, float(LANES)), axis=1, keepdims=True)
        hot = lane_f == idx
        tops.append(mx)
        hots.append(hot)
        sel = jnp.where(hot, 1.0, sel)
        work = jnp.where(hot, -jnp.inf, work)
    es = [jnp.exp(t - tops[0]) for t in tops]
    tot = es[0] + es[1] + es[2] + es[3]

    r_id = lax.broadcasted_iota(jnp.int32, (tm, tm), 0)
    c_id = lax.broadcasted_iota(jnp.int32, (tm, tm), 1)
    strict = jnp.where(c_id < r_id, 1.0, 0.0).astype(BF16)
    rank_all = jnp.dot(strict, sel.astype(BF16), preferred_element_type=F32) + carry_ref[...]
    carry_ref[...] = carry_ref[...] + jnp.sum(sel, axis=0, keepdims=True)
    cnt_ref[...] = carry_ref[...]

    eidx = jnp.zeros((tm, LANES), jnp.int32)
    gates = jnp.zeros((tm, LANES), F32)
    rank = jnp.zeros((tm, LANES), jnp.int32)
    for kk in range(TOP_K):
        e_k = jnp.sum(jnp.where(hots[kk], lane_f, 0.0), axis=1, keepdims=True)
        r_k = jnp.sum(jnp.where(hots[kk], rank_all, 0.0), axis=1, keepdims=True)
        eidx = jnp.where(lane == kk, e_k.astype(jnp.int32), eidx)
        gates = jnp.where(lane == kk, es[kk] / tot, gates)
        rank = jnp.where(lane == kk, r_k.astype(jnp.int32), rank)
    eidx_ref[0] = eidx
    gates_ref[0] = gates
    rank_ref[0] = rank


def _post1(x, y, gpost, gpre, mod6, rw_hi, rw_lo, rb, tm=256):
    B = x.shape[0]
    blk = pl.BlockSpec((1, tm, D_MODEL), lambda b, i: (b, i, 0))
    vec = pl.BlockSpec((1, D_MODEL), lambda b, i: (0, 0))
    lan = pl.BlockSpec((1, tm, LANES), lambda b, i: (b, i, 0))
    rw = pl.BlockSpec((D_MODEL, LANES), lambda b, i: (0, 0))

    def mod_spec(k):
        return pl.BlockSpec((1, 1, 1, D_MODEL), lambda b, i: (k, b, 0, 0))

    lan_i = jax.ShapeDtypeStruct((B, SEQ, LANES), jnp.int32)
    return pl.pallas_call(
        _post1_kernel,
        grid=(B, SEQ // tm),
        in_specs=[blk, blk, vec, vec, mod_spec(2), mod_spec(4), mod_spec(3), rw, rw,
                  pl.BlockSpec((1, LANES), lambda b, i: (0, 0))],
        out_specs=[blk, pl.BlockSpec((1, tm, D_MODEL // 2), lambda b, i: (b, i, 0)), lan, lan, lan,
                   pl.BlockSpec((1, LANES), lambda b, i: (0, 0))],
        out_shape=[jax.ShapeDtypeStruct((B, SEQ, D_MODEL), F32),
                   jax.ShapeDtypeStruct((B, SEQ, D_MODEL // 2), jnp.uint32),
                   lan_i, jax.ShapeDtypeStruct((B, SEQ, LANES), F32), lan_i,
                   jax.ShapeDtypeStruct((1, LANES), F32)],
        scratch_shapes=[pltpu.VMEM((1, LANES), F32)],
        compiler_params=_params(("arbitrary", "arbitrary")),
        name="post_mix",
    )(x, y, gpost.reshape(1, D_MODEL), gpre.reshape(1, D_MODEL), mod6, mod6, mod6, rw_hi, rw_lo, rb)


DMA_ISSUE_UNROLL = 8


def _gather_kernel(idx_ref, tv_ref, src_ref, o_ref, land_ref, sem):
    i = pl.program_id(0)
    n = pl.num_programs(0)
    slot = i % 2

    def issue(t, sl):
        base = t * MOE_TILE

        def start(j, c):
            for prio in range(2):
                r = 2 * j + prio
                pltpu.make_async_copy(src_ref.at[pl.ds(idx_ref[base + r], 1)], land_ref.at[sl, pl.ds(r, 1)],
                                      sem.at[sl]).start(priority=prio)
            return c

        lax.fori_loop(0, MOE_TILE // 2, start, 0, unroll=DMA_ISSUE_UNROLL // 2)

    @pl.when(i == 0)
    def _():
        issue(0, 0)

    @pl.when(jnp.logical_and(i + 1 < n, tv_ref[jnp.minimum(i + 1, n - 1)] > 0))
    def _():
        issue(i + 1, 1 - slot)

    @pl.when(tv_ref[i] > 0)
    def _():
        pltpu.make_async_copy(src_ref.at[pl.ds(0, MOE_TILE)], land_ref.at[slot], sem.at[slot]).wait()
        o_ref[...] = land_ref[slot]

    @pl.when(tv_ref[i] == 0)
    def _():
        o_ref[...] = jnp.zeros_like(o_ref)


def _gather_rows(slot_tok, tile_v, src):
    n = slot_tok.shape[0]
    return pl.pallas_call(
        _gather_kernel,
        grid_spec=pltpu.PrefetchScalarGridSpec(
            num_scalar_prefetch=2,
            grid=(n // MOE_TILE,),
            in_specs=[pl.BlockSpec(memory_space=pl.ANY)],
            out_specs=pl.BlockSpec((MOE_TILE, D_MODEL // 2), lambda i, idx, tv: (i, 0)),
            scratch_shapes=[pltpu.VMEM((2, MOE_TILE, D_MODEL // 2), jnp.uint32),
                            pltpu.SemaphoreType.DMA((2,))]),
        out_shape=jax.ShapeDtypeStruct((n, D_MODEL // 2), src.dtype),
        compiler_params=_params(("arbitrary",)),
        name="moe_gather",
    )(slot_tok, tile_v, src)


def _stream_expert_weights(te_ref, fr_ref, ne_ref, lg_ref, copies, cast):
    p = pl.program_id(0)
    i = pl.program_id(1)

    @pl.when(fr_ref[i] == 1)
    def _():
        @pl.when(jnp.logical_and(p == 0, i == 0))
        def _():
            for c in copies(te_ref[i], p):
                c.start()

        for c in copies(te_ref[i], p):
            c.wait()
        cast()
        is_last_group = lg_ref[i] == 1

        @pl.when(jnp.logical_not(jnp.logical_and(is_last_group, p == pl.num_programs(0) - 1)))
        def _():
            for c in copies(ne_ref[i], p + lg_ref[i]):
                c.start()


def _gmm1_kernel(te_ref, tv_ref, fr_ref, ne_ref, lg_ref, x_ref, bg_ref, bu_ref, w_hbm, o_ref,
                 land_ref, wb_ref, sem):
    i = pl.program_id(1)
    tn = o_ref.shape[1]

    def copies(e, p):
        c0 = pl.multiple_of(p * tn, tn)
        return [pltpu.make_async_copy(w_hbm.at[e, :, pl.ds(c0 + off, tn)], land_ref.at[s], sem.at[s])
                for s, off in enumerate((0, D_EXPERT))]

    def cast():
        for s in range(2):
            wb_ref[s] = land_ref[s].astype(BF16)

    _stream_expert_weights(te_ref, fr_ref, ne_ref, lg_ref, copies, cast)

    @pl.when(tv_ref[i] > 0)
    def _():
        lo, hi = _unpack_bf16_pair(x_ref[...])
        half = D_MODEL // 2
        glu = (jnp.dot(lo, wb_ref[0, :half, :], preferred_element_type=F32)
               + jnp.dot(hi, wb_ref[0, half:, :], preferred_element_type=F32) + bg_ref[0])
        lin = (jnp.dot(lo, wb_ref[1, :half, :], preferred_element_type=F32)
               + jnp.dot(hi, wb_ref[1, half:, :], preferred_element_type=F32) + bu_ref[0])
        glu = jnp.minimum(glu, SWIGLU_LIMIT)
        lin = jnp.clip(lin, -SWIGLU_LIMIT, SWIGLU_LIMIT)
        o_ref[...] = (glu * jax.nn.sigmoid(SWIGLU_ALPHA * glu) * (lin + 1.0)).astype(o_ref.dtype)

    @pl.when(tv_ref[i] == 0)
    def _():
        o_ref[...] = jnp.zeros_like(o_ref)


def _gmm1(tiles, xs, w_gu, b_gu, tn=512):
    n_slots = xs.shape[0]
    n_tiles = n_slots // MOE_TILE
    nf = D_EXPERT // tn
    b3 = b_gu.reshape(N_EXPERTS, 1, 2 * D_EXPERT)
    return pl.pallas_call(
        _gmm1_kernel,
        grid_spec=pltpu.PrefetchScalarGridSpec(
            num_scalar_prefetch=5,
            grid=(nf, n_tiles),
            in_specs=[pl.BlockSpec((MOE_TILE, D_MODEL // 2), lambda n, i, te, *_: (i, 0)),
                      pl.BlockSpec((1, 1, tn), lambda n, i, te, *_: (te[i], 0, n)),
                      pl.BlockSpec((1, 1, tn), lambda n, i, te, *_: (te[i], 0, nf + n)),
                      pl.BlockSpec(memory_space=pl.ANY)],
            out_specs=pl.BlockSpec((MOE_TILE, tn), lambda n, i, te, *_: (i, n)),
            scratch_shapes=[pltpu.VMEM((2, D_MODEL, tn), F32), pltpu.VMEM((2, D_MODEL, tn), BF16),
                            pltpu.SemaphoreType.DMA((2,))]),
        out_shape=jax.ShapeDtypeStruct((n_slots, D_EXPERT), BF16),
        compiler_params=_params(("arbitrary", "arbitrary")),
        name="moe_gate_up",
    )(*tiles, xs, b3, b3, w_gu)


def _gmm2_kernel(te_ref, tv_ref, fr_ref, ne_ref, lg_ref, a_ref, b_ref, w_hbm, o_ref, land_ref, wb_ref, sem):
    i = pl.program_id(1)
    tn = wb_ref.shape[1]

    def copies(e, p):
        return [pltpu.make_async_copy(w_hbm.at[e, :, pl.ds(pl.multiple_of(p * tn, tn), tn)], land_ref, sem)]

    def cast():
        wb_ref[...] = land_ref[...].astype(BF16)

    _stream_expert_weights(te_ref, fr_ref, ne_ref, lg_ref, copies, cast)

    @pl.when(tv_ref[i] > 0)
    def _():
        y = jnp.dot(a_ref[...], wb_ref[...], preferred_element_type=F32) + b_ref[0]
        half = y.shape[1] // 2
        o_ref[...] = _pack_bf16_pair(y[:, :half], y[:, half:])

    @pl.when(tv_ref[i] == 0)
    def _():
        o_ref[...] = jnp.zeros_like(o_ref)


def _gmm2(tiles, act, w_down, b_down, tn):
    n_slots = act.shape[0]
    n_tiles = n_slots // MOE_TILE
    nn = D_MODEL // tn
    return pl.pallas_call(
        _gmm2_kernel,
        grid_spec=pltpu.PrefetchScalarGridSpec(
            num_scalar_prefetch=5,
            grid=(nn, n_tiles),
            in_specs=[pl.BlockSpec((MOE_TILE, D_EXPERT), lambda n, i, te, *_: (i, 0)),
                      pl.BlockSpec((1, 1, tn), lambda n, i, te, *_: (te[i], 0, n)),
                      pl.BlockSpec(memory_space=pl.ANY)],
            out_specs=pl.BlockSpec((MOE_TILE, tn // 2), lambda n, i, te, *_: (i, n)),
            scratch_shapes=[pltpu.VMEM((D_EXPERT, tn), F32), pltpu.VMEM((D_EXPERT, tn), BF16),
                            pltpu.SemaphoreType.DMA]),
        out_shape=jax.ShapeDtypeStruct((n_slots, D_MODEL // 2), jnp.uint32),
        compiler_params=_params(("arbitrary", "arbitrary")),
        name="moe_down",
    )(*tiles, act, b_down.reshape(N_EXPERTS, 1, D_MODEL), w_down)


COMBINE_TM = 256
GMM2_TN = 2048


def _combine_kernel(dest_ref, x1_ref, gates_ref, gpost_ref, g2_ref, ys_ref, o_ref, buf_ref, sem):
    tm = COMBINE_TM
    n_rows = tm * TOP_K
    n_steps = pl.num_programs(0) * pl.num_programs(1)
    step = pl.program_id(0) * pl.num_programs(1) + pl.program_id(1)
    slot = step % 2

    def issue(st, sl):
        base = st * n_rows

        def start(j, c):
            for prio in range(2):
                r = 2 * j + prio
                pltpu.make_async_copy(ys_ref.at[pl.ds(dest_ref[base + r], 1)],
                                      buf_ref.at[sl, pl.ds(r, 1)], sem.at[sl]).start(priority=prio)
            return c

        lax.fori_loop(0, n_rows // 2, start, 0, unroll=DMA_ISSUE_UNROLL // 2)

    @pl.when(step == 0)
    def _():
        issue(0, 0)

    @pl.when(step + 1 < n_steps)
    def _():
        issue(step + 1, 1 - slot)

    pltpu.make_async_copy(ys_ref.at[pl.ds(0, n_rows)], buf_ref.at[slot], sem.at[slot]).wait()

    gates = gates_ref[0]
    lane = lax.broadcasted_iota(jnp.int32, gates.shape, 1)
    half = GMM2_TN // 2
    parts = [None] * (2 * (D_MODEL // GMM2_TN))
    for kk in range(TOP_K):
        gk = jnp.sum(jnp.where(lane == kk, gates, 0.0), axis=1, keepdims=True)
        rows = buf_ref[slot, pl.ds(kk * tm, tm), :]
        for n in range(D_MODEL // GMM2_TN):
            lo, hi = _unpack_bf16_pair(rows[:, n * half:(n + 1) * half])
            for t, val in ((2 * n, lo), (2 * n + 1, hi)):
                term = gk * val.astype(F32)
                parts[t] = term if parts[t] is None else parts[t] + term
    f = jnp.concatenate(parts, axis=1)
    fn = f * lax.rsqrt(jnp.mean(f * f, axis=-1, keepdims=True) + NORM_EPS) * gpost_ref[...]
    o_ref[0] = x1_ref[0] + g2_ref[0, 0] * fn


def _combine(dest, x1, gates, gpost, mod6, ys):
    B = x1.shape[0]
    tm = COMBINE_TM
    blk = pl.BlockSpec((1, tm, D_MODEL), lambda b, i, d: (b, i, 0))
    return pl.pallas_call(
        _combine_kernel,
        grid_spec=pltpu.PrefetchScalarGridSpec(
            num_scalar_prefetch=1,
            grid=(B, SEQ // tm),
            in_specs=[blk,
                      pl.BlockSpec((1, tm, LANES), lambda b, i, d: (b, i, 0)),
                      pl.BlockSpec((1, D_MODEL), lambda b, i, d: (0, 0)),
                      pl.BlockSpec((1, 1, 1, D_MODEL), lambda b, i, d: (5, b, 0, 0)),
                      pl.BlockSpec(memory_space=pl.ANY)],
            out_specs=blk,
            scratch_shapes=[pltpu.VMEM((2, tm * TOP_K, D_MODEL // 2), jnp.uint32),
                            pltpu.SemaphoreType.DMA((2,))]),
        out_shape=jax.ShapeDtypeStruct((B, SEQ, D_MODEL), F32),
        compiler_params=_params(("arbitrary", "arbitrary")),
        name="moe_combine",
    )(dest, x1, gates, gpost.reshape(1, D_MODEL), mod6, ys)


def _rope_tables():
    rows = SEQ // GRID_W
    r, col = jnp.meshgrid(jnp.arange(rows, dtype=F32), jnp.arange(GRID_W, dtype=F32), indexing="ij")
    n_freq = A_ROPE // 4
    inv = ROPE_BASE ** (-jnp.arange(n_freq, dtype=F32) / n_freq)
    ang = jnp.concatenate([r.reshape(-1, 1) * inv, col.reshape(-1, 1) * inv], axis=-1)
    cos = jnp.concatenate([jnp.repeat(jnp.cos(ang), 2, axis=-1), jnp.ones((CTX_LEN, A_ROPE), F32)], axis=0)
    sin = jnp.concatenate([jnp.repeat(jnp.sin(ang), 2, axis=-1), jnp.zeros((CTX_LEN, A_ROPE), F32)], axis=0)
    pad = ((0, 0), (0, LANES - A_ROPE))
    return jnp.concatenate([cos, sin], axis=-1)[None], jnp.pad(cos, pad)[None], jnp.pad(sin, pad)[None]


def _rot_partner(w):
    wp = w.reshape(*w.shape[:-1], A_ROPE // 2, 2)
    return jnp.stack([-wp[..., 1], wp[..., 0]], axis=-1).reshape(w.shape)


def _layer(x, ctx, mod6, tabs, norm_pre_mix, norm_post_mix, norm_pre_ffn, norm_post_ffn, w_in, b_gates,
           m_out_norm, q_norm, kv_norm, w_uq, w_ukv, w_branch_a, w_branch_b, w_out, router_w, router_b,
           w_gu, b_gu, w_down, b_down):
    B = x.shape[0]

    tab, cos_k, sin_k = tabs
    w_in_t = w_in.T
    gate_bias = jnp.concatenate([b_gates, jnp.zeros((LANES - N_GATE_COLS,), F32)]).reshape(1, LANES)
    wq = w_uq.reshape(Q_LORA, A_HEADS, A_NOPE + A_ROPE)
    wq_r = wq[..., A_NOPE:]
    w_uq2 = jnp.concatenate([wq[..., :A_NOPE], wq_r, _rot_partner(wq_r)], axis=-1)
    w_uq2 = w_uq2.reshape(Q_LORA, A_HEADS * A_QW).astype(BF16)

    h_all = _prenorm(x, ctx, norm_pre_mix, mod6)

    z_a = _matmul(h_all, w_in_t, rows=ROWS_ALL, tm=1152, tn=512, n_out=OFF_G, out_dtype=BF16, wt_row0=OFF_Q,
                  name="in_proj_a")
    z_b = _matmul(h_all, w_in_t, rows=ROWS_ALL, tm=1152, tn=512, n_out=Q_LORA + KV_LORA, out_dtype=BF16,
                  wt_row0=OFF_CQ, name="in_proj_b")
    z_d = _matmul(h_all, w_in_t, rows=SEQ, tm=1024, tn=512, n_out=2 * D_MODEL, out_dtype=BF16, wt_row0=OFF_GA,
                  name="in_proj_d")
    zg, krz = _small_proj(h_all, w_in_t, gate_bias, cos_k, sin_k, tm=1152)

    h_f, h_b = _mlstm(z_a, zg)
    ya = _mlstm_out(h_f, h_b, z_a, m_out_norm)

    q = _matmul(z_b, w_uq2, rows=SEQ, tm=1024, tn=1024, n_out=A_HEADS * A_QW, out_dtype=BF16, k=Q_LORA,
                rms_gain=q_norm, rope_tab=tab, scale=A_SCALE, name="q_up")
    kv = _matmul(z_b, w_ukv, rows=ROWS_ALL, tm=1152, tn=1024, n_out=A_HEADS * (A_NOPE + A_DV), out_dtype=BF16,
                 k=KV_LORA, x_col=Q_LORA // KV_LORA, rms_gain=kv_norm, name="kv_up")
    yb = _attention(q, kv, krz)

    ua = _matmul(ya, w_branch_a, rows=SEQ, tm=1024, tn=512, n_out=D_MODEL, out_dtype=BF16, gate=z_d,
                 name="branch_a")
    u = _matmul(yb, w_branch_b, rows=SEQ, tm=1024, tn=512, n_out=D_MODEL, out_dtype=BF16, gate=z_d,
                gate_col=D_MODEL // 512, add=ua, name="branch_b")
    y = _matmul(u, w_out, rows=SEQ, tm=1024, tn=512, n_out=D_MODEL, out_dtype=BF16, name="out_proj")

    rw = jnp.concatenate([router_w, jnp.zeros((D_MODEL, LANES - N_EXPERTS), F32)], axis=1)
    rw_hi = rw.astype(BF16)
    rw_lo = (rw - rw_hi.astype(F32)).astype(BF16)
    rb = jnp.concatenate([router_b, jnp.zeros((LANES - N_EXPERTS,), F32)]).reshape(1, LANES)
    x1, h2p, eidx, gates, rank, counts = _post1(x, y, norm_post_mix, norm_pre_ffn, mod6, rw_hi, rw_lo, rb)

    T = B * SEQ
    n_tiles = -(-(T * TOP_K + N_EXPERTS * (MOE_TILE - 1)) // MOE_TILE)
    n_slots = n_tiles * MOE_TILE
    cnt = counts[0, :N_EXPERTS].astype(jnp.int32)
    padded = (cnt + MOE_TILE - 1) // MOE_TILE * MOE_TILE
    g_end = jnp.cumsum(padded)
    g_start = g_end - padded
    e_flat = eidx.reshape(T, LANES)[:, :TOP_K]
    dest = g_start[e_flat] + rank.reshape(T, LANES)[:, :TOP_K]
    tok = jnp.broadcast_to(jnp.arange(T, dtype=jnp.int32)[:, None], (T, TOP_K))
    slot_tok = jnp.zeros((n_slots,), jnp.int32).at[dest.reshape(-1)].set(tok.reshape(-1))
    tile_lo = jnp.arange(n_tiles, dtype=jnp.int32) * MOE_TILE
    tile_e = jnp.minimum(jnp.sum((g_end[None, :] <= tile_lo[:, None]).astype(jnp.int32), axis=1), N_EXPERTS - 1)
    tile_v = (tile_lo < g_end[-1]).astype(jnp.int32)
    prev_e = jnp.concatenate([jnp.full((1,), -1, jnp.int32), tile_e[:-1]])
    tile_first = (tile_v * (tile_e != prev_e)).astype(jnp.int32)
    after = g_end[tile_e] // MOE_TILE
    has_after = after * MOE_TILE < g_end[-1]
    tile_next_e = jnp.where(has_after, tile_e[jnp.minimum(after, n_tiles - 1)], tile_e[0]).astype(jnp.int32)
    tile_last = jnp.logical_not(has_after).astype(jnp.int32)
    tiles = (tile_e, tile_v, tile_first, tile_next_e, tile_last)
    dest_t = dest.reshape(T // COMBINE_TM, COMBINE_TM, TOP_K).transpose(0, 2, 1).reshape(-1)

    xs = _gather_rows(slot_tok, tile_v, h2p.reshape(T, D_MODEL // 2))
    act = _gmm1(tiles, xs, w_gu, b_gu)
    ys = _gmm2(tiles, act, w_down, b_down, tn=GMM2_TN)
    return _combine(dest_t, x1, gates, norm_post_ffn, mod6, ys)


def kernel(x, c, ctx, c_ctx, w_ada, b_ada, norm_pre_mix, norm_post_mix, norm_pre_ffn, norm_post_ffn, w_in,
           b_gates, m_out_norm, q_norm, kv_norm, w_uq, w_ukv, w_branch_a, w_branch_b, w_out, router_w, router_b,
           w_gu, b_gu, w_down, b_down):
    B = x.shape[0]
    depth = w_ada.shape[0]
    assert depth == 1, "context-stream outputs between layers are not implemented"
    tabs = _rope_tables()
    cvec = jnp.concatenate([c, c_ctx[None], jnp.zeros((8 - B - 1, D_MODEL), F32)], axis=0)
    l = 0
    mod = _ada(cvec, w_ada[l], b_ada[l])
    mod6 = mod.reshape(8, N_MOD, 1, D_MODEL).transpose(1, 0, 2, 3)
    return _layer(x, ctx, mod6, tabs, norm_pre_mix[l], norm_post_mix[l], norm_pre_ffn[l], norm_post_ffn[l],
                  w_in[l], b_gates[l], m_out_norm[l], q_norm[l], kv_norm[l], w_uq[l], w_ukv[l], w_branch_a[l],
                  w_branch_b[l], w_out[l], router_w[l], router_b[l], w_gu[l], b_gu[l], w_down[l], b_down[l])
```

```python
import functools

import jax
import jax.numpy as jnp
import numpy as np
from jax import lax
from jax.experimental import pallas as pl
from jax.experimental.pallas import tpu as pltpu

F32 = jnp.float32
BF16 = jnp.bfloat16

D_MODEL = 4096
SEQ = 2048
GRID_W = 64
CTX_LEN = 256
ROWS_ALL = SEQ + CTX_LEN
NORM_EPS = 1e-6
N_MOD = 6

M_HEADS = 8
M_DQK = 256
M_DV = 512
M_QK_W = M_HEADS * M_DQK
M_V_W = M_HEADS * M_DV
N_GATE_COLS = 4 * M_HEADS
M_CHUNK = 256

A_HEADS = 32
A_NOPE = 128
A_ROPE = 64
A_DV = 128
Q_LORA = 1024
KV_LORA = 512
ROPE_BASE = 10000.0
A_SCALE = (A_NOPE + A_ROPE) ** -0.5
A_QW = 256

N_EXPERTS = 32
TOP_K = 4
D_EXPERT = 1536
SWIGLU_LIMIT = 7.0
SWIGLU_ALPHA = 1.702
MOE_TILE = 384

OFF_Q = 0
OFF_G = 2 * M_QK_W + 2 * M_V_W
OFF_CQ = OFF_G + N_GATE_COLS
OFF_CKV = OFF_CQ + Q_LORA
OFF_KR = OFF_CKV + KV_LORA
OFF_GA = OFF_KR + A_ROPE
OFF_GB = OFF_GA + D_MODEL

LANES = 128
V7X_VMEM_LIMIT = 56 * 1024 * 1024


def _params(sem, vmem=V7X_VMEM_LIMIT):
    return pltpu.CompilerParams(dimension_semantics=sem, vmem_limit_bytes=vmem)


def _split3(x):
    hi = x.astype(BF16)
    r1 = x - hi.astype(F32)
    mid = r1.astype(BF16)
    lo = (r1 - mid.astype(F32)).astype(BF16)
    return hi, mid, lo


def _ada_kernel(c_ref, w_ref, b_ref, o_ref):
    c = c_ref[...]
    cs = c * jax.nn.sigmoid(c)
    c_hi, c_mid, _ = _split3(cs)
    w = w_ref[...]
    w_hi = w.astype(BF16)
    w_lo = (w - w_hi.astype(F32)).astype(BF16)
    acc = jnp.dot(c_hi, w_hi, preferred_element_type=F32)
    acc += jnp.dot(c_mid, w_hi, preferred_element_type=F32)
    acc += jnp.dot(c_hi, w_lo, preferred_element_type=F32)
    o_ref[...] = acc + b_ref[...]


def _ada(cvec, w_ada, b_ada):
    n = w_ada.shape[1]
    tn = 1024
    return pl.pallas_call(
        _ada_kernel,
        grid=(n // tn,),
        in_specs=[pl.BlockSpec((8, D_MODEL), lambda j: (0, 0)),
                  pl.BlockSpec((D_MODEL, tn), lambda j: (0, j)),
                  pl.BlockSpec((1, tn), lambda j: (0, j))],
        out_specs=pl.BlockSpec((8, tn), lambda j: (0, j)),
        out_shape=jax.ShapeDtypeStruct((8, n), F32),
        compiler_params=_params(("arbitrary",)),
        name="ada_mod",
    )(cvec, w_ada, b_ada.reshape(1, n))


def _prenorm_kernel(x_ref, c_ref, g_ref, sc_ref, sh_ref, o_ref):
    j = pl.program_id(1)
    nlat = SEQ // CTX_LEN

    def run(src):
        xf = src[0]
        ms = jnp.mean(xf * xf, axis=-1, keepdims=True)
        y = xf * lax.rsqrt(ms + NORM_EPS) * g_ref[...]
        o_ref[0] = (y * (1.0 + sc_ref[0, 0]) + sh_ref[0, 0]).astype(o_ref.dtype)

    @pl.when(j < nlat)
    def _():
        run(x_ref)

    @pl.when(j == nlat)
    def _():
        run(c_ref)


def _prenorm(x, ctx, gain, mod6):
    B = x.shape[0]
    nlat = SEQ // CTX_LEN
    blk = (1, CTX_LEN, D_MODEL)

    def mod_spec(k):
        return pl.BlockSpec((1, 1, 1, D_MODEL),
                            lambda b, j: (k, jnp.where(j == nlat, B, b), 0, 0))

    return pl.pallas_call(
        _prenorm_kernel,
        grid=(B, nlat + 1),
        in_specs=[pl.BlockSpec(blk, lambda b, j: (b, jnp.minimum(j, nlat - 1), 0)),
                  pl.BlockSpec(blk, lambda b, j: (b, 0, 0)),
                  pl.BlockSpec((1, D_MODEL), lambda b, j: (0, 0)),
                  mod_spec(1), mod_spec(0)],
        out_specs=pl.BlockSpec(blk, lambda b, j: (b, j, 0)),
        out_shape=jax.ShapeDtypeStruct((B, ROWS_ALL, D_MODEL), BF16),
        compiler_params=_params(("arbitrary", "arbitrary")),
        name="prenorm",
    )(x, ctx, gain.reshape(1, D_MODEL), mod6, mod6)


def _mm_kernel(*refs, rms, gate, add, rope, scale, w_transposed):
    refs = list(refs)
    x_ref = refs.pop(0)
    w_ref = refs.pop(0)
    gain_ref = refs.pop(0) if rms else None
    gate_ref = refs.pop(0) if gate else None
    add_ref = refs.pop(0) if add else None
    tab_ref = refs.pop(0) if rope else None
    o_ref = refs.pop(0)
    xn_ref = refs.pop(0) if rms else None

    if rms:
        @pl.when(pl.program_id(2) == 0)
        def _():
            xf = x_ref[0].astype(F32)
            ms = jnp.mean(xf * xf, axis=-1, keepdims=True)
            xn_ref[...] = (xf * lax.rsqrt(ms + NORM_EPS) * gain_ref[...]).astype(BF16)
        xv = xn_ref[...]
    else:
        xv = x_ref[0]
    contract = (((1,), (1,)), ((), ())) if w_transposed else (((1,), (0,)), ((), ()))
    acc = lax.dot_general(xv, w_ref[...].astype(BF16), contract, preferred_element_type=F32)
    if gate:
        acc = jax.nn.sigmoid(gate_ref[0].astype(F32)) * acc
    if add:
        acc = acc + add_ref[0].astype(F32)
    if rope:
        tab = tab_ref[0]
        for hh in range(acc.shape[1] // A_QW):
            c0 = hh * A_QW
            r = acc[:, c0 + LANES:c0 + A_QW] * tab
            r = r + pltpu.roll(r, A_ROPE, axis=1)
            o_ref[0, :, c0:c0 + LANES] = (acc[:, c0:c0 + LANES] * scale).astype(o_ref.dtype)
            o_ref[0, :, c0 + LANES:c0 + A_QW] = (r * scale).astype(o_ref.dtype)
    else:
        o_ref[0] = acc.astype(o_ref.dtype)


def _matmul(x, w, *, rows, tm, tn, n_out, out_dtype, x_col=0, k=None, w_col=0, wt_row0=None,
            rms_gain=None, gate=None, gate_col=0, add=None, rope_tab=None, scale=1.0, name="mm"):
    B = x.shape[0]
    k = x.shape[2] if k is None else k
    assert rows % tm == 0 and n_out % tn == 0
    if wt_row0 is None:
        assert w.shape[0] == k
        w_spec = pl.BlockSpec((k, tn), lambda b, i, j: (0, j + w_col))
    else:
        assert w.shape[1] == k and wt_row0 % 8 == 0
        w_spec = pl.BlockSpec((pl.Element(tn), pl.Element(k)),
                              lambda b, i, j: (pl.multiple_of(wt_row0 + j * tn, 8), 0))
    in_specs = [pl.BlockSpec((1, tm, k), lambda b, i, j: (b, i, x_col)), w_spec]
    args = [x, w]
    scratch = []
    if rms_gain is not None:
        in_specs.append(pl.BlockSpec((1, k), lambda b, i, j: (0, 0)))
        args.append(rms_gain.reshape(1, k))
        scratch.append(pltpu.VMEM((tm, k), BF16))
    if gate is not None:
        in_specs.append(pl.BlockSpec((1, tm, tn), lambda b, i, j: (b, i, j + gate_col)))
        args.append(gate)
    if add is not None:
        in_specs.append(pl.BlockSpec((1, tm, tn), lambda b, i, j: (b, i, j)))
        args.append(add)
    if rope_tab is not None:
        in_specs.append(pl.BlockSpec((1, tm, LANES), lambda b, i, j: (0, i, 0)))
        args.append(rope_tab)
    kern = functools.partial(_mm_kernel, rms=rms_gain is not None, gate=gate is not None, add=add is not None,
                             rope=rope_tab is not None, scale=scale, w_transposed=wt_row0 is not None)
    return pl.pallas_call(
        kern,
        grid=(B, rows // tm, n_out // tn),
        in_specs=in_specs,
        out_specs=pl.BlockSpec((1, tm, tn), lambda b, i, j: (b, i, j)),
        out_shape=jax.ShapeDtypeStruct((B, rows, n_out), out_dtype),
        scratch_shapes=scratch,
        compiler_params=_params(("arbitrary", "arbitrary", "arbitrary")),
        name=name,
    )(*args)


def _small_kernel(x_ref, wg_ref, wk_ref, bias_ref, cos_ref, sin_ref, zg_ref, kr_ref):
    xv = x_ref[0]

    def proj(w_ref):
        w = w_ref[...]
        w_hi = w.astype(BF16)
        w_lo = (w - w_hi.astype(F32)).astype(BF16)
        nt = (((1,), (1,)), ((), ()))
        return (lax.dot_general(xv, w_hi, nt, preferred_element_type=F32)
                + lax.dot_general(xv, w_lo, nt, preferred_element_type=F32))

    zg_ref[0] = proj(wg_ref) + bias_ref[...]
    kr = proj(wk_ref)
    lane = lax.broadcasted_iota(jnp.int32, kr.shape, 1)
    partner = jnp.where((lane & 1) == 0, -pltpu.roll(kr, LANES - 1, axis=1), pltpu.roll(kr, 1, axis=1))
    kr_ref[0] = (kr * cos_ref[0] + partner * sin_ref[0]).astype(kr_ref.dtype)


def _small_proj(h_all, w_in_t, bias, cos_k, sin_k, tm):
    B = h_all.shape[0]
    blk = lambda c: pl.BlockSpec((1, tm, c), lambda b, i: (b, i, 0))
    tab = pl.BlockSpec((1, tm, LANES), lambda b, i: (0, i, 0))
    w_rows = lambda r0: pl.BlockSpec((pl.Element(LANES), pl.Element(D_MODEL)), lambda b, i: (r0, 0))
    return pl.pallas_call(
        _small_kernel,
        grid=(B, ROWS_ALL // tm),
        in_specs=[blk(D_MODEL), w_rows(OFF_G), w_rows(OFF_KR),
                  pl.BlockSpec((1, LANES), lambda b, i: (0, 0)),
                  tab, tab],
        out_specs=[blk(LANES), blk(LANES)],
        out_shape=[jax.ShapeDtypeStruct((B, ROWS_ALL, LANES), F32),
                   jax.ShapeDtypeStruct((B, ROWS_ALL, LANES), BF16)],
        compiler_params=_params(("arbitrary", "arbitrary")),
        name="small_proj",
    )(h_all, w_in_t, w_in_t, bias, cos_k, sin_k)


def _log_sigmoid(x):
    return jnp.minimum(x, 0.0) - jnp.log1p(jnp.exp(-jnp.abs(x)))


MLSTM_HP = 4


def _mlstm_dir(q_ref, k_ref, v_ref, zg_ref, o_ref, c_ref, n_ref, m_ref, *, reverse, head0):
    L = M_CHUNK
    zg = zg_ref[0]
    lane = lax.broadcasted_iota(jnp.int32, (L, LANES), 1)
    sub = lax.broadcasted_iota(jnp.int32, (LANES, L), 0)
    row_id = lax.broadcasted_iota(jnp.int32, (L, L), 0)
    col_id = lax.broadcasted_iota(jnp.int32, (L, L), 1)
    seen = (col_id >= row_id) if reverse else (col_id <= row_id)

    lf = _log_sigmoid(zg)
    tri = jnp.where(seen, 1.0, 0.0).astype(BF16)
    hi, mid, lo = _split3(lf)
    bcum = (jnp.dot(tri, hi, preferred_element_type=F32)
            + jnp.dot(tri, mid, preferred_element_type=F32)
            + jnp.dot(tri, lo, preferred_element_type=F32))

    bcum_t = bcum.T
    zg_t = zg.T

    def pick_col(a, c):
        return jnp.sum(jnp.where(lane == c, a, 0.0), axis=1, keepdims=True)

    def pick_row(a_t, c):
        return jnp.sum(jnp.where(sub == c, a_t, 0.0), axis=0, keepdims=True)

    for hh in range(MLSTM_HP):
        col_i = head0 + hh + (2 * M_HEADS if reverse else 0)
        col_f = col_i + M_HEADS
        q = q_ref[0, :, hh * M_DQK:(hh + 1) * M_DQK]
        k = k_ref[0, :, hh * M_DQK:(hh + 1) * M_DQK]
        v = v_ref[0, :, hh * M_DV:(hh + 1) * M_DV]
        c_prev = c_ref[hh]
        n_prev = n_ref[hh]
        m_prev = m_ref[hh]

        b_col = pick_col(bcum, col_f)
        i_col = pick_col(zg, col_i)
        b_row = pick_row(bcum_t, col_f)
        i_row = pick_row(zg_t, col_i)
        b_last = b_col[0:1, :] if reverse else b_col[L - 1:L, :]

        d = jnp.where(seen, b_col - b_row + i_row, -jnp.inf)
        inter = b_col + m_prev
        m_t = jnp.maximum(inter, jnp.max(d, axis=1, keepdims=True))
        w = jnp.exp(d - m_t) * (M_DQK ** -0.5)
        scale = jnp.exp(inter - m_t)
        s = lax.dot_general(q, k, (((1,), (1,)), ((), ())), preferred_element_type=F32) * w
        cq = lax.dot_general(q, c_prev.astype(BF16), (((1,), (1,)), ((), ())),
                             preferred_element_type=F32)
        num = jnp.dot(s.astype(BF16), v, preferred_element_type=F32) + scale * cq
        nq = jnp.sum(q.astype(F32) * n_prev, axis=1, keepdims=True)
        den = jnp.sum(s, axis=1, keepdims=True) + scale * nq
        h_out = num / jnp.maximum(jnp.abs(den), jnp.exp(-m_t))
        o_ref[0, :, hh * M_DV:(hh + 1) * M_DV] = h_out.astype(o_ref.dtype)

        g = b_last - b_col + i_col
        m_new = jnp.maximum(b_last + m_prev, jnp.max(g, axis=0, keepdims=True))
        wk = jnp.exp(g - m_new) * (M_DQK ** -0.5)
        dec = jnp.exp(b_last + m_prev - m_new)
        vw = (v.astype(F32) * wk).astype(BF16)
        upd = lax.dot_general(vw, k, (((0,), (0,)), ((), ())), preferred_element_type=F32)
        c_ref[hh] = dec * c_prev + upd
        n_ref[hh] = dec * n_prev + jnp.sum(k.astype(F32) * wk, axis=0, keepdims=True)
        m_ref[hh] = m_new


def _mlstm_kernel(qf, kf, vf, gf, qb, kb, vb, gb, of, ob, cf, nf, mf, cb, nb, mb):
    head0 = pl.program_id(1) * MLSTM_HP

    @pl.when(pl.program_id(2) == 0)
    def _():
        for r in (cf, nf, mf, cb, nb, mb):
            r[...] = jnp.zeros_like(r)

    _mlstm_dir(qf, kf, vf, gf, of, cf, nf, mf, reverse=False, head0=head0)
    _mlstm_dir(qb, kb, vb, gb, ob, cb, nb, mb, reverse=True, head0=head0)


def _mlstm(z_a, zg):
    B = z_a.shape[0]
    L = M_CHUNK
    HP = MLSTM_HP
    nlat = SEQ // L
    assert CTX_LEN == L and M_HEADS % HP == 0
    k_blk0 = M_QK_W // (HP * M_DQK)
    v_blk0 = (2 * M_QK_W) // (HP * M_DV)

    def cf(j):
        return jnp.where(j == 0, nlat, j - 1)

    def cb(j):
        return jnp.where(j == 0, nlat, nlat - j)

    def specs(ch):
        return [pl.BlockSpec((1, L, HP * M_DQK), lambda b, h, j: (b, ch(j), h)),
                pl.BlockSpec((1, L, HP * M_DQK), lambda b, h, j: (b, ch(j), k_blk0 + h)),
                pl.BlockSpec((1, L, HP * M_DV), lambda b, h, j: (b, ch(j), v_blk0 + h)),
                pl.BlockSpec((1, L, LANES), lambda b, h, j: (b, ch(j), 0))]

    out_sds = jax.ShapeDtypeStruct((B, SEQ, M_V_W), BF16)
    state = [pltpu.VMEM((HP, M_DV, M_DQK), F32), pltpu.VMEM((HP, 1, M_DQK), F32),
             pltpu.VMEM((HP, 1, 1), F32)]
    return pl.pallas_call(
        _mlstm_kernel,
        grid=(B, M_HEADS // HP, nlat + 1),
        in_specs=specs(cf) + specs(cb),
        out_specs=[pl.BlockSpec((1, L, HP * M_DV), lambda b, h, j: (b, jnp.maximum(j - 1, 0), h)),
                   pl.BlockSpec((1, L, HP * M_DV), lambda b, h, j: (b, jnp.minimum(nlat - j, nlat - 1), h))],
        out_shape=[out_sds, out_sds],
        scratch_shapes=state + state,
        compiler_params=_params(("arbitrary", "arbitrary", "arbitrary")),
        name="mlstm_scan",
    )(z_a, z_a, z_a, zg, z_a, z_a, z_a, zg)


def _mlstm_out_kernel(hf_ref, hb_ref, zo_ref, g_ref, o_ref):
    for h in range(M_HEADS):
        sl = slice(h * M_DV, (h + 1) * M_DV)
        hs = hf_ref[0, :, sl].astype(F32) + hb_ref[0, :, sl].astype(F32)
        ms = jnp.mean(hs * hs, axis=-1, keepdims=True)
        hn = hs * lax.rsqrt(ms + NORM_EPS) * g_ref[:, sl]
        o_ref[0, :, sl] = (hn * jax.nn.sigmoid(zo_ref[0, :, sl].astype(F32))).astype(o_ref.dtype)


def _mlstm_out(h_f, h_b, z_a, gain, tm=256):
    B = h_f.shape[0]
    blk = pl.BlockSpec((1, tm, M_V_W), lambda b, i: (b, i, 0))
    return pl.pallas_call(
        _mlstm_out_kernel,
        grid=(B, SEQ // tm),
        in_specs=[blk, blk,
                  pl.BlockSpec((1, tm, M_V_W), lambda b, i: (b, i, (2 * M_QK_W + M_V_W) // M_V_W)),
                  pl.BlockSpec((1, M_V_W), lambda b, i: (0, 0))],
        out_specs=blk,
        out_shape=jax.ShapeDtypeStruct((B, SEQ, M_V_W), BF16),
        compiler_params=_params(("arbitrary", "arbitrary")),
        name="mlstm_out",
    )(h_f, h_b, z_a, gain.reshape(1, M_V_W))


ATTN_KEY_BLOCK = 768


ATTN_HP = 2


def _attn_kernel(q_ref, kv_ref, kr_ref, o_ref):
    kvw = A_NOPE + A_DV
    m = [None] * ATTN_HP
    l = [None] * ATTN_HP
    acc = [None] * ATTN_HP
    for j in range(ROWS_ALL // ATTN_KEY_BLOCK):
        ks = slice(j * ATTN_KEY_BLOCK, (j + 1) * ATTN_KEY_BLOCK)
        kr = kr_ref[0, ks, :]
        for hh in range(ATTN_HP):
            q = q_ref[0, :, hh * A_QW:(hh + 1) * A_QW]
            kcat = jnp.concatenate([kv_ref[0, ks, hh * kvw:hh * kvw + A_NOPE], kr], axis=1)
            s = lax.dot_general(q, kcat, (((1,), (1,)), ((), ())), preferred_element_type=F32)
            m_j = jnp.max(s, axis=1, keepdims=True)
            m_new = m_j if j == 0 else jnp.maximum(m[hh], m_j)
            p = jnp.exp(s - m_new)
            l_j = jnp.sum(p, axis=1, keepdims=True)
            pv = jnp.dot(p.astype(BF16), kv_ref[0, ks, hh * kvw + A_NOPE:(hh + 1) * kvw],
                         preferred_element_type=F32)
            if j == 0:
                l[hh], acc[hh] = l_j, pv
            else:
                alpha = jnp.exp(m[hh] - m_new)
                l[hh] = alpha * l[hh] + l_j
                acc[hh] = alpha * acc[hh] + pv
            m[hh] = m_new
    for hh in range(ATTN_HP):
        o_ref[0, :, hh * A_DV:(hh + 1) * A_DV] = (acc[hh] / l[hh]).astype(o_ref.dtype)


def _attention(q, kv, krz, tq=1024):
    B = q.shape[0]
    HP = ATTN_HP
    return pl.pallas_call(
        _attn_kernel,
        grid=(B, A_HEADS // HP, SEQ // tq),
        in_specs=[pl.BlockSpec((1, tq, HP * A_QW), lambda b, h, i: (b, i, h)),
                  pl.BlockSpec((1, ROWS_ALL, HP * (A_NOPE + A_DV)), lambda b, h, i: (b, 0, h)),
                  pl.BlockSpec((1, ROWS_ALL, LANES), lambda b, h, i: (b, 0, 0))],
        out_specs=pl.BlockSpec((1, tq, HP * A_DV), lambda b, h, i: (b, i, h)),
        out_shape=jax.ShapeDtypeStruct((B, SEQ, A_HEADS * A_DV), BF16),
        compiler_params=_params(("arbitrary", "arbitrary", "arbitrary")),
        name="latent_attn",
    )(q, kv, krz)


def _pack_bf16_pair(a, b):
    ua = pltpu.bitcast(a.astype(BF16).astype(F32), jnp.uint32)
    ub = pltpu.bitcast(b.astype(BF16).astype(F32), jnp.uint32)
    return (ua >> 16) | (ub & jnp.uint32(0xFFFF0000))


def _unpack_bf16_pair(u):
    a = pltpu.bitcast(u << 16, F32).astype(BF16)
    b = pltpu.bitcast(u & jnp.uint32(0xFFFF0000), F32).astype(BF16)
    return a, b


def _post1_kernel(x_ref, y_ref, gpost_ref, gpre_ref, g1_ref, sc_ref, sh_ref, rwh_ref, rwl_ref, rb_ref,
                  x1_ref, h2_ref, eidx_ref, gates_ref, rank_ref, cnt_ref, carry_ref):
    first = jnp.logical_and(pl.program_id(0) == 0, pl.program_id(1) == 0)

    @pl.when(first)
    def _():
        carry_ref[...] = jnp.zeros_like(carry_ref)

    y = y_ref[0].astype(F32)
    yn = y * lax.rsqrt(jnp.mean(y * y, axis=-1, keepdims=True) + NORM_EPS) * gpost_ref[...]
    x1 = x_ref[0] + g1_ref[0, 0] * yn
    x1_ref[0] = x1
    hn = x1 * lax.rsqrt(jnp.mean(x1 * x1, axis=-1, keepdims=True) + NORM_EPS) * gpre_ref[...]
    h2 = hn * (1.0 + sc_ref[0, 0]) + sh_ref[0, 0]
    half = D_MODEL // 2
    h2_ref[0] = _pack_bf16_pair(h2[:, :half], h2[:, half:])

    h_hi = h2.astype(BF16)
    h_lo = (h2 - h_hi.astype(F32)).astype(BF16)
    logits = (jnp.dot(h_hi, rwh_ref[...], preferred_element_type=F32)
              + jnp.dot(h_lo, rwh_ref[...], preferred_element_type=F32)
              + jnp.dot(h_hi, rwl_ref[...], preferred_element_type=F32)) + rb_ref[...]
    tm = logits.shape[0]
    lane = lax.broadcasted_iota(jnp.int32, (tm, LANES), 1)
    lane_f = lane.astype(F32)
    work = jnp.where(lane < N_EXPERTS, logits, -jnp.inf)
    sel = jnp.zeros((tm, LANES), F32)
    tops, hots = [], []
    for _ in range(TOP_K):
        mx = jnp.max(work, axis=1, keepdims=True)
        idx = jnp.min(jnp.where(work == mx, lane_f, float(LANES)), axis=1, keepdims=True)
        hot = lane_f == idx
        tops.append(mx)
        hots.append(hot)
        sel = jnp.where(hot, 1.0, sel)
        work = jnp.where(hot, -jnp.inf, work)
    es = [jnp.exp(t - tops[0]) for t in tops]
    tot = es[0] + es[1] + es[2] + es[3]

    r_id = lax.broadcasted_iota(jnp.int32, (tm, tm), 0)
    c_id = lax.broadcasted_iota(jnp.int32, (tm, tm), 1)
    strict = jnp.where(c_id < r_id, 1.0, 0.0).astype(BF16)
    rank_all = jnp.dot(strict, sel.astype(BF16), preferred_element_type=F32) + carry_ref[...]
    carry_ref[...] = carry_ref[...] + jnp.sum(sel, axis=0, keepdims=True)
    cnt_ref[...] = carry_ref[...]

    eidx = jnp.zeros((tm, LANES), jnp.int32)
    gates = jnp.zeros((tm, LANES), F32)
    rank = jnp.zeros((tm, LANES), jnp.int32)
    for kk in range(TOP_K):
        e_k = jnp.sum(jnp.where(hots[kk], lane_f, 0.0), axis=1, keepdims=True)
        r_k = jnp.sum(jnp.where(hots[kk], rank_all, 0.0), axis=1, keepdims=True)
        eidx = jnp.where(lane == kk, e_k.astype(jnp.int32), eidx)
        gates = jnp.where(lane == kk, es[kk] / tot, gates)
        rank = jnp.where(lane == kk, r_k.astype(jnp.int32), rank)
    eidx_ref[0] = eidx
    gates_ref[0] = gates
    rank_ref[0] = rank


def _post1(x, y, gpost, gpre, mod6, rw_hi, rw_lo, rb, tm=256):
    B = x.shape[0]
    blk = pl.BlockSpec((1, tm, D_MODEL), lambda b, i: (b, i, 0))
    vec = pl.BlockSpec((1, D_MODEL), lambda b, i: (0, 0))
    lan = pl.BlockSpec((1, tm, LANES), lambda b, i: (b, i, 0))
    rw = pl.BlockSpec((D_MODEL, LANES), lambda b, i: (0, 0))

    def mod_spec(k):
        return pl.BlockSpec((1, 1, 1, D_MODEL), lambda b, i: (k, b, 0, 0))

    lan_i = jax.ShapeDtypeStruct((B, SEQ, LANES), jnp.int32)
    return pl.pallas_call(
        _post1_kernel,
        grid=(B, SEQ // tm),
        in_specs=[blk, blk, vec, vec, mod_spec(2), mod_spec(4), mod_spec(3), rw, rw,
                  pl.BlockSpec((1, LANES), lambda b, i: (0, 0))],
        out_specs=[blk, pl.BlockSpec((1, tm, D_MODEL // 2), lambda b, i: (b, i, 0)), lan, lan, lan,
                   pl.BlockSpec((1, LANES), lambda b, i: (0, 0))],
        out_shape=[jax.ShapeDtypeStruct((B, SEQ, D_MODEL), F32),
                   jax.ShapeDtypeStruct((B, SEQ, D_MODEL // 2), jnp.uint32),
                   lan_i, jax.ShapeDtypeStruct((B, SEQ, LANES), F32), lan_i,
                   jax.ShapeDtypeStruct((1, LANES), F32)],
        scratch_shapes=[pltpu.VMEM((1, LANES), F32)],
        compiler_params=_params(("arbitrary", "arbitrary")),
        name="post_mix",
    )(x, y, gpost.reshape(1, D_MODEL), gpre.reshape(1, D_MODEL), mod6, mod6, mod6, rw_hi, rw_lo, rb)


DMA_ISSUE_UNROLL = 8


def _gather_kernel(idx_ref, tv_ref, src_ref, o_ref, land_ref, sem):
    i = pl.program_id(0)
    n = pl.num_programs(0)
    slot = i % 2

    def issue(t, sl):
        base = t * MOE_TILE

        def start(j, c):
            for prio in range(2):
                r = 2 * j + prio
                pltpu.make_async_copy(src_ref.at[pl.ds(idx_ref[base + r], 1)], land_ref.at[sl, pl.ds(r, 1)],
                                      sem.at[sl]).start(priority=prio)
            return c

        lax.fori_loop(0, MOE_TILE // 2, start, 0, unroll=DMA_ISSUE_UNROLL // 2)

    @pl.when(i == 0)
    def _():
        issue(0, 0)

    @pl.when(jnp.logical_and(i + 1 < n, tv_ref[jnp.minimum(i + 1, n - 1)] > 0))
    def _():
        issue(i + 1, 1 - slot)

    @pl.when(tv_ref[i] > 0)
    def _():
        pltpu.make_async_copy(src_ref.at[pl.ds(0, MOE_TILE)], land_ref.at[slot], sem.at[slot]).wait()
        o_ref[...] = land_ref[slot]

    @pl.when(tv_ref[i] == 0)
    def _():
        o_ref[...] = jnp.zeros_like(o_ref)


def _gather_rows(slot_tok, tile_v, src):
    n = slot_tok.shape[0]
    return pl.pallas_call(
        _gather_kernel,
        grid_spec=pltpu.PrefetchScalarGridSpec(
            num_scalar_prefetch=2,
            grid=(n // MOE_TILE,),
            in_specs=[pl.BlockSpec(memory_space=pl.ANY)],
            out_specs=pl.BlockSpec((MOE_TILE, D_MODEL // 2), lambda i, idx, tv: (i, 0)),
            scratch_shapes=[pltpu.VMEM((2, MOE_TILE, D_MODEL // 2), jnp.uint32),
                            pltpu.SemaphoreType.DMA((2,))]),
        out_shape=jax.ShapeDtypeStruct((n, D_MODEL // 2), src.dtype),
        compiler_params=_params(("arbitrary",)),
        name="moe_gather",
    )(slot_tok, tile_v, src)


MOE_ROW_STEP = 128


def _for_row_count(rows, run, o_ref):
    for m in range(MOE_ROW_STEP, MOE_TILE + 1, MOE_ROW_STEP):
        @pl.when(jnp.logical_and(rows > m - MOE_ROW_STEP, rows <= m))
        def _(m=m):
            run(m)
            if m < MOE_TILE:
                o_ref[m:, :] = jnp.zeros((MOE_TILE - m, o_ref.shape[1]), o_ref.dtype)

    @pl.when(rows == 0)
    def _():
        o_ref[...] = jnp.zeros_like(o_ref)


def _stream_expert_weights(te_ref, fr_ref, ne_ref, lg_ref, copies, cast):
    p = pl.program_id(0)
    i = pl.program_id(1)

    @pl.when(fr_ref[i] == 1)
    def _():
        @pl.when(jnp.logical_and(p == 0, i == 0))
        def _():
            for c in copies(te_ref[i], p):
                c.start()

        for c in copies(te_ref[i], p):
            c.wait()
        cast()
        is_last_group = lg_ref[i] == 1

        @pl.when(jnp.logical_not(jnp.logical_and(is_last_group, p == pl.num_programs(0) - 1)))
        def _():
            for c in copies(ne_ref[i], p + lg_ref[i]):
                c.start()


def _gmm1_kernel(te_ref, tv_ref, fr_ref, ne_ref, lg_ref, x_ref, bg_ref, bu_ref, w_hbm, o_ref,
                 land_ref, wb_ref, sem):
    i = pl.program_id(1)
    tn = o_ref.shape[1]

    def copies(e, p):
        c0 = pl.multiple_of(p * tn, tn)
        return [pltpu.make_async_copy(w_hbm.at[e, :, pl.ds(c0 + off, tn)], land_ref.at[s], sem.at[s])
                for s, off in enumerate((0, D_EXPERT))]

    def cast():
        for s in range(2):
            wb_ref[s] = land_ref[s].astype(BF16)

    _stream_expert_weights(te_ref, fr_ref, ne_ref, lg_ref, copies, cast)

    def run(m):
        lo, hi = _unpack_bf16_pair(x_ref[:m, :])
        half = D_MODEL // 2
        glu = (jnp.dot(lo, wb_ref[0, :half, :], preferred_element_type=F32)
               + jnp.dot(hi, wb_ref[0, half:, :], preferred_element_type=F32) + bg_ref[0])
        lin = (jnp.dot(lo, wb_ref[1, :half, :], preferred_element_type=F32)
               + jnp.dot(hi, wb_ref[1, half:, :], preferred_element_type=F32) + bu_ref[0])
        glu = jnp.minimum(glu, SWIGLU_LIMIT)
        lin = jnp.clip(lin, -SWIGLU_LIMIT, SWIGLU_LIMIT)
        o_ref[:m, :] = (glu * jax.nn.sigmoid(SWIGLU_ALPHA * glu) * (lin + 1.0)).astype(o_ref.dtype)

    _for_row_count(tv_ref[i], run, o_ref)


def _gmm1(tiles, xs, w_gu, b_gu, tn=512):
    n_slots = xs.shape[0]
    n_tiles = n_slots // MOE_TILE
    nf = D_EXPERT // tn
    b3 = b_gu.reshape(N_EXPERTS, 1, 2 * D_EXPERT)
    return pl.pallas_call(
        _gmm1_kernel,
        grid_spec=pltpu.PrefetchScalarGridSpec(
            num_scalar_prefetch=5,
            grid=(nf, n_tiles),
            in_specs=[pl.BlockSpec((MOE_TILE, D_MODEL // 2), lambda n, i, te, *_: (i, 0)),
                      pl.BlockSpec((1, 1, tn), lambda n, i, te, *_: (te[i], 0, n)),
                      pl.BlockSpec((1, 1, tn), lambda n, i, te, *_: (te[i], 0, nf + n)),
                      pl.BlockSpec(memory_space=pl.ANY)],
            out_specs=pl.BlockSpec((MOE_TILE, tn), lambda n, i, te, *_: (i, n)),
            scratch_shapes=[pltpu.VMEM((2, D_MODEL, tn), F32), pltpu.VMEM((2, D_MODEL, tn), BF16),
                            pltpu.SemaphoreType.DMA((2,))]),
        out_shape=jax.ShapeDtypeStruct((n_slots, D_EXPERT), BF16),
        compiler_params=_params(("arbitrary", "arbitrary")),
        name="moe_gate_up",
    )(*tiles, xs, b3, b3, w_gu)


def _gmm2_kernel(te_ref, tv_ref, fr_ref, ne_ref, lg_ref, a_ref, b_ref, w_hbm, o_ref, land_ref, wb_ref, sem):
    i = pl.program_id(1)
    tn = wb_ref.shape[1]

    def copies(e, p):
        return [pltpu.make_async_copy(w_hbm.at[e, :, pl.ds(pl.multiple_of(p * tn, tn), tn)], land_ref, sem)]

    def cast():
        wb_ref[...] = land_ref[...].astype(BF16)

    _stream_expert_weights(te_ref, fr_ref, ne_ref, lg_ref, copies, cast)

    def run(m):
        y = jnp.dot(a_ref[:m, :], wb_ref[...], preferred_element_type=F32) + b_ref[0]
        half = y.shape[1] // 2
        o_ref[:m, :] = _pack_bf16_pair(y[:, :half], y[:, half:])

    _for_row_count(tv_ref[i], run, o_ref)


def _gmm2(tiles, act, w_down, b_down, tn):
    n_slots = act.shape[0]
    n_tiles = n_slots // MOE_TILE
    nn = D_MODEL // tn
    return pl.pallas_call(
        _gmm2_kernel,
        grid_spec=pltpu.PrefetchScalarGridSpec(
            num_scalar_prefetch=5,
            grid=(nn, n_tiles),
            in_specs=[pl.BlockSpec((MOE_TILE, D_EXPERT), lambda n, i, te, *_: (i, 0)),
                      pl.BlockSpec((1, 1, tn), lambda n, i, te, *_: (te[i], 0, n)),
                      pl.BlockSpec(memory_space=pl.ANY)],
            out_specs=pl.BlockSpec((MOE_TILE, tn // 2), lambda n, i, te, *_: (i, n)),
            scratch_shapes=[pltpu.VMEM((D_EXPERT, tn), F32), pltpu.VMEM((D_EXPERT, tn), BF16),
                            pltpu.SemaphoreType.DMA]),
        out_shape=jax.ShapeDtypeStruct((n_slots, D_MODEL // 2), jnp.uint32),
        compiler_params=_params(("arbitrary", "arbitrary")),
        name="moe_down",
    )(*tiles, act, b_down.reshape(N_EXPERTS, 1, D_MODEL), w_down)


COMBINE_TM = 256
GMM2_TN = 2048


def _combine_kernel(dest_ref, x1_ref, gates_ref, gpost_ref, g2_ref, ys_ref, o_ref, buf_ref, sem):
    tm = COMBINE_TM
    n_rows = tm * TOP_K
    n_steps = pl.num_programs(0) * pl.num_programs(1)
    step = pl.program_id(0) * pl.num_programs(1) + pl.program_id(1)
    slot = step % 2

    def issue(st, sl):
        base = st * n_rows

        def start(j, c):
            for prio in range(2):
                r = 2 * j + prio
                pltpu.make_async_copy(ys_ref.at[pl.ds(dest_ref[base + r], 1)],
                                      buf_ref.at[sl, pl.ds(r, 1)], sem.at[sl]).start(priority=prio)
            return c

        lax.fori_loop(0, n_rows // 2, start, 0, unroll=DMA_ISSUE_UNROLL // 2)

    @pl.when(step == 0)
    def _():
        issue(0, 0)

    @pl.when(step + 1 < n_steps)
    def _():
        issue(step + 1, 1 - slot)

    pltpu.make_async_copy(ys_ref.at[pl.ds(0, n_rows)], buf_ref.at[slot], sem.at[slot]).wait()

    gates = gates_ref[0]
    lane = lax.broadcasted_iota(jnp.int32, gates.shape, 1)
    half = GMM2_TN // 2
    parts = [None] * (2 * (D_MODEL // GMM2_TN))
    for kk in range(TOP_K):
        gk = jnp.sum(jnp.where(lane == kk, gates, 0.0), axis=1, keepdims=True)
        rows = buf_ref[slot, pl.ds(kk * tm, tm), :]
        for n in range(D_MODEL // GMM2_TN):
            lo, hi = _unpack_bf16_pair(rows[:, n * half:(n + 1) * half])
            for t, val in ((2 * n, lo), (2 * n + 1, hi)):
                term = gk * val.astype(F32)
                parts[t] = term if parts[t] is None else parts[t] + term
    f = jnp.concatenate(parts, axis=1)
    fn = f * lax.rsqrt(jnp.mean(f * f, axis=-1, keepdims=True) + NORM_EPS) * gpost_ref[...]
    o_ref[0] = x1_ref[0] + g2_ref[0, 0] * fn


def _combine(dest, x1, gates, gpost, mod6, ys):
    B = x1.shape[0]
    tm = COMBINE_TM
    blk = pl.BlockSpec((1, tm, D_MODEL), lambda b, i, d: (b, i, 0))
    return pl.pallas_call(
        _combine_kernel,
        grid_spec=pltpu.PrefetchScalarGridSpec(
            num_scalar_prefetch=1,
            grid=(B, SEQ // tm),
            in_specs=[blk,
                      pl.BlockSpec((1, tm, LANES), lambda b, i, d: (b, i, 0)),
                      pl.BlockSpec((1, D_MODEL), lambda b, i, d: (0, 0)),
                      pl.BlockSpec((1, 1, 1, D_MODEL), lambda b, i, d: (5, b, 0, 0)),
                      pl.BlockSpec(memory_space=pl.ANY)],
            out_specs=blk,
            scratch_shapes=[pltpu.VMEM((2, tm * TOP_K, D_MODEL // 2), jnp.uint32),
                            pltpu.SemaphoreType.DMA((2,))]),
        out_shape=jax.ShapeDtypeStruct((B, SEQ, D_MODEL), F32),
        compiler_params=_params(("arbitrary", "arbitrary")),
        name="moe_combine",
    )(dest, x1, gates, gpost.reshape(1, D_MODEL), mod6, ys)


def _rope_tables():
    rows = SEQ // GRID_W
    r, col = jnp.meshgrid(jnp.arange(rows, dtype=F32), jnp.arange(GRID_W, dtype=F32), indexing="ij")
    n_freq = A_ROPE // 4
    inv = ROPE_BASE ** (-jnp.arange(n_freq, dtype=F32) / n_freq)
    ang = jnp.concatenate([r.reshape(-1, 1) * inv, col.reshape(-1, 1) * inv], axis=-1)
    cos = jnp.concatenate([jnp.repeat(jnp.cos(ang), 2, axis=-1), jnp.ones((CTX_LEN, A_ROPE), F32)], axis=0)
    sin = jnp.concatenate([jnp.repeat(jnp.sin(ang), 2, axis=-1), jnp.zeros((CTX_LEN, A_ROPE), F32)], axis=0)
    pad = ((0, 0), (0, LANES - A_ROPE))
    return jnp.concatenate([cos, sin], axis=-1)[None], jnp.pad(cos, pad)[None], jnp.pad(sin, pad)[None]


def _rot_partner(w):
    wp = w.reshape(*w.shape[:-1], A_ROPE // 2, 2)
    return jnp.stack([-wp[..., 1], wp[..., 0]], axis=-1).reshape(w.shape)


def _layer(x, ctx, mod6, tabs, norm_pre_mix, norm_post_mix, norm_pre_ffn, norm_post_ffn, w_in, b_gates,
           m_out_norm, q_norm, kv_norm, w_uq, w_ukv, w_branch_a, w_branch_b, w_out, router_w, router_b,
           w_gu, b_gu, w_down, b_down):
    B = x.shape[0]

    tab, cos_k, sin_k = tabs
    w_in_t = w_in.T
    gate_bias = jnp.concatenate([b_gates, jnp.zeros((LANES - N_GATE_COLS,), F32)]).reshape(1, LANES)
    wq = w_uq.reshape(Q_LORA, A_HEADS, A_NOPE + A_ROPE)
    wq_r = wq[..., A_NOPE:]
    w_uq2 = jnp.concatenate([wq[..., :A_NOPE], wq_r, _rot_partner(wq_r)], axis=-1)
    w_uq2 = w_uq2.reshape(Q_LORA, A_HEADS * A_QW).astype(BF16)

    h_all = _prenorm(x, ctx, norm_pre_mix, mod6)

    z_a = _matmul(h_all, w_in_t, rows=ROWS_ALL, tm=1152, tn=512, n_out=OFF_G, out_dtype=BF16, wt_row0=OFF_Q,
                  name="in_proj_a")
    z_b = _matmul(h_all, w_in_t, rows=ROWS_ALL, tm=1152, tn=512, n_out=Q_LORA + KV_LORA, out_dtype=BF16,
                  wt_row0=OFF_CQ, name="in_proj_b")
    z_d = _matmul(h_all, w_in_t, rows=SEQ, tm=1024, tn=512, n_out=2 * D_MODEL, out_dtype=BF16, wt_row0=OFF_GA,
                  name="in_proj_d")
    zg, krz = _small_proj(h_all, w_in_t, gate_bias, cos_k, sin_k, tm=1152)

    h_f, h_b = _mlstm(z_a, zg)
    ya = _mlstm_out(h_f, h_b, z_a, m_out_norm)

    q = _matmul(z_b, w_uq2, rows=SEQ, tm=1024, tn=1024, n_out=A_HEADS * A_QW, out_dtype=BF16, k=Q_LORA,
                rms_gain=q_norm, rope_tab=tab, scale=A_SCALE, name="q_up")
    kv = _matmul(z_b, w_ukv, rows=ROWS_ALL, tm=1152, tn=1024, n_out=A_HEADS * (A_NOPE + A_DV), out_dtype=BF16,
                 k=KV_LORA, x_col=Q_LORA // KV_LORA, rms_gain=kv_norm, name="kv_up")
    yb = _attention(q, kv, krz)

    ua = _matmul(ya, w_branch_a, rows=SEQ, tm=1024, tn=512, n_out=D_MODEL, out_dtype=BF16, gate=z_d,
                 name="branch_a")
    u = _matmul(yb, w_branch_b, rows=SEQ, tm=1024, tn=512, n_out=D_MODEL, out_dtype=BF16, gate=z_d,
                gate_col=D_MODEL // 512, add=ua, name="branch_b")
    y = _matmul(u, w_out, rows=SEQ, tm=1024, tn=512, n_out=D_MODEL, out_dtype=BF16, name="out_proj")

    rw = jnp.concatenate([router_w, jnp.zeros((D_MODEL, LANES - N_EXPERTS), F32)], axis=1)
    rw_hi = rw.astype(BF16)
    rw_lo = (rw - rw_hi.astype(F32)).astype(BF16)
    rb = jnp.concatenate([router_b, jnp.zeros((LANES - N_EXPERTS,), F32)]).reshape(1, LANES)
    x1, h2p, eidx, gates, rank, counts = _post1(x, y, norm_post_mix, norm_pre_ffn, mod6, rw_hi, rw_lo, rb)

    T = B * SEQ
    n_tiles = -(-(T * TOP_K + N_EXPERTS * (MOE_TILE - 1)) // MOE_TILE)
    n_slots = n_tiles * MOE_TILE
    cnt = counts[0, :N_EXPERTS].astype(jnp.int32)
    padded = (cnt + MOE_TILE - 1) // MOE_TILE * MOE_TILE
    g_end = jnp.cumsum(padded)
    g_start = g_end - padded
    e_flat = eidx.reshape(T, LANES)[:, :TOP_K]
    dest = g_start[e_flat] + rank.reshape(T, LANES)[:, :TOP_K]
    tok = jnp.broadcast_to(jnp.arange(T, dtype=jnp.int32)[:, None], (T, TOP_K))
    slot_tok = jnp.zeros((n_slots,), jnp.int32).at[dest.reshape(-1)].set(tok.reshape(-1))
    tile_lo = jnp.arange(n_tiles, dtype=jnp.int32) * MOE_TILE
    tile_e = jnp.minimum(jnp.sum((g_end[None, :] <= tile_lo[:, None]).astype(jnp.int32), axis=1), N_EXPERTS - 1)
    tile_v = jnp.clip(g_start[tile_e] + cnt[tile_e] - tile_lo, 0, MOE_TILE) * (tile_lo < g_end[-1])
    tile_v = tile_v.astype(jnp.int32)
    prev_e = jnp.concatenate([jnp.full((1,), -1, jnp.int32), tile_e[:-1]])
    tile_first = jnp.logical_and(tile_v > 0, tile_e != prev_e).astype(jnp.int32)
    after = g_end[tile_e] // MOE_TILE
    has_after = after * MOE_TILE < g_end[-1]
    tile_next_e = jnp.where(has_after, tile_e[jnp.minimum(after, n_tiles - 1)], tile_e[0]).astype(jnp.int32)
    tile_last = jnp.logical_not(has_after).astype(jnp.int32)
    tiles = (tile_e, tile_v, tile_first, tile_next_e, tile_last)
    dest_t = dest.reshape(T // COMBINE_TM, COMBINE_TM, TOP_K).transpose(0, 2, 1).reshape(-1)

    xs = _gather_rows(slot_tok, tile_v, h2p.reshape(T, D_MODEL // 2))
    act = _gmm1(tiles, xs, w_gu, b_gu)
    ys = _gmm2(tiles, act, w_down, b_down, tn=GMM2_TN)
    return _combine(dest_t, x1, gates, norm_post_ffn, mod6, ys)


def kernel(x, c, ctx, c_ctx, w_ada, b_ada, norm_pre_mix, norm_post_mix, norm_pre_ffn, norm_post_ffn, w_in,
           b_gates, m_out_norm, q_norm, kv_norm, w_uq, w_ukv, w_branch_a, w_branch_b, w_out, router_w, router_b,
           w_gu, b_gu, w_down, b_down):
    B = x.shape[0]
    depth = w_ada.shape[0]
    assert depth == 1, "context-stream outputs between layers are not implemented"
    tabs = _rope_tables()
    cvec = jnp.concatenate([c, c_ctx[None], jnp.zeros((8 - B - 1, D_MODEL), F32)], axis=0)
    l = 0
    mod = _ada(cvec, w_ada[l], b_ada[l])
    mod6 = mod.reshape(8, N_MOD, 1, D_MODEL).transpose(1, 0, 2, 3)
    return _layer(x, ctx, mod6, tabs, norm_pre_mix[l], norm_post_mix[l], norm_pre_ffn[l], norm_post_ffn[l],
                  w_in[l], b_gates[l], m_out_norm[l], q_norm[l], kv_norm[l], w_uq[l], w_ukv[l], w_branch_a[l],
                  w_branch_b[l], w_out[l], router_w[l], router_b[l], w_gu[l], b_gu[l], w_down[l], b_down[l])
```

```python
import functools

import jax
import jax.numpy as jnp
import numpy as np
from jax import lax
from jax.experimental import pallas as pl
from jax.experimental.pallas import tpu as pltpu

F32 = jnp.float32
BF16 = jnp.bfloat16

D_MODEL = 4096
SEQ = 2048
GRID_W = 64
CTX_LEN = 256
ROWS_ALL = SEQ + CTX_LEN
NORM_EPS = 1e-6
N_MOD = 6

M_HEADS = 8
M_DQK = 256
M_DV = 512
M_QK_W = M_HEADS * M_DQK
M_V_W = M_HEADS * M_DV
N_GATE_COLS = 4 * M_HEADS
M_CHUNK = 256

A_HEADS = 32
A_NOPE = 128
A_ROPE = 64
A_DV = 128
Q_LORA = 1024
KV_LORA = 512
ROPE_BASE = 10000.0
A_SCALE = (A_NOPE + A_ROPE) ** -0.5
A_QW = 256

N_EXPERTS = 32
TOP_K = 4
D_EXPERT = 1536
SWIGLU_LIMIT = 7.0
SWIGLU_ALPHA = 1.702
MOE_TILE = 512

OFF_Q = 0
OFF_G = 2 * M_QK_W + 2 * M_V_W
OFF_CQ = OFF_G + N_GATE_COLS
OFF_CKV = OFF_CQ + Q_LORA
OFF_KR = OFF_CKV + KV_LORA
OFF_GA = OFF_KR + A_ROPE
OFF_GB = OFF_GA + D_MODEL

LANES = 128
V7X_VMEM_LIMIT = 56 * 1024 * 1024


def _params(sem, vmem=V7X_VMEM_LIMIT):
    return pltpu.CompilerParams(dimension_semantics=sem, vmem_limit_bytes=vmem)


def _split3(x):
    hi = x.astype(BF16)
    r1 = x - hi.astype(F32)
    mid = r1.astype(BF16)
    lo = (r1 - mid.astype(F32)).astype(BF16)
    return hi, mid, lo


def _ada_kernel(c_ref, w_ref, b_ref, o_ref):
    c = c_ref[...]
    cs = c * jax.nn.sigmoid(c)
    c_hi, c_mid, _ = _split3(cs)
    w = w_ref[...]
    w_hi = w.astype(BF16)
    w_lo = (w - w_hi.astype(F32)).astype(BF16)
    acc = jnp.dot(c_hi, w_hi, preferred_element_type=F32)
    acc += jnp.dot(c_mid, w_hi, preferred_element_type=F32)
    acc += jnp.dot(c_hi, w_lo, preferred_element_type=F32)
    o_ref[...] = acc + b_ref[...]


def _ada(cvec, w_ada, b_ada):
    n = w_ada.shape[1]
    tn = 1024
    return pl.pallas_call(
        _ada_kernel,
        grid=(n // tn,),
        in_specs=[pl.BlockSpec((8, D_MODEL), lambda j: (0, 0)),
                  pl.BlockSpec((D_MODEL, tn), lambda j: (0, j)),
                  pl.BlockSpec((1, tn), lambda j: (0, j))],
        out_specs=pl.BlockSpec((8, tn), lambda j: (0, j)),
        out_shape=jax.ShapeDtypeStruct((8, n), F32),
        compiler_params=_params(("arbitrary",)),
        name="ada_mod",
    )(cvec, w_ada, b_ada.reshape(1, n))


def _prenorm_kernel(x_ref, c_ref, g_ref, sc_ref, sh_ref, o_ref):
    j = pl.program_id(1)
    nlat = SEQ // CTX_LEN

    def run(src):
        xf = src[0]
        ms = jnp.mean(xf * xf, axis=-1, keepdims=True)
        y = xf * lax.rsqrt(ms + NORM_EPS) * g_ref[...]
        o_ref[0] = (y * (1.0 + sc_ref[0, 0]) + sh_ref[0, 0]).astype(o_ref.dtype)

    @pl.when(j < nlat)
    def _():
        run(x_ref)

    @pl.when(j == nlat)
    def _():
        run(c_ref)


def _prenorm(x, ctx, gain, mod6):
    B = x.shape[0]
    nlat = SEQ // CTX_LEN
    blk = (1, CTX_LEN, D_MODEL)

    def mod_spec(k):
        return pl.BlockSpec((1, 1, 1, D_MODEL),
                            lambda b, j: (k, jnp.where(j == nlat, B, b), 0, 0))

    return pl.pallas_call(
        _prenorm_kernel,
        grid=(B, nlat + 1),
        in_specs=[pl.BlockSpec(blk, lambda b, j: (b, jnp.minimum(j, nlat - 1), 0)),
                  pl.BlockSpec(blk, lambda b, j: (b, 0, 0)),
                  pl.BlockSpec((1, D_MODEL), lambda b, j: (0, 0)),
                  mod_spec(1), mod_spec(0)],
        out_specs=pl.BlockSpec(blk, lambda b, j: (b, j, 0)),
        out_shape=jax.ShapeDtypeStruct((B, ROWS_ALL, D_MODEL), BF16),
        compiler_params=_params(("arbitrary", "arbitrary")),
        name="prenorm",
    )(x, ctx, gain.reshape(1, D_MODEL), mod6, mod6)


def _mm_kernel(*refs, rms, gate, add, rope, scale, w_transposed):
    refs = list(refs)
    x_ref = refs.pop(0)
    w_ref = refs.pop(0)
    gain_ref = refs.pop(0) if rms else None
    gate_ref = refs.pop(0) if gate else None
    add_ref = refs.pop(0) if add else None
    tab_ref = refs.pop(0) if rope else None
    o_ref = refs.pop(0)
    xn_ref = refs.pop(0) if rms else None

    if rms:
        @pl.when(pl.program_id(2) == 0)
        def _():
            xf = x_ref[0].astype(F32)
            ms = jnp.mean(xf * xf, axis=-1, keepdims=True)
            xn_ref[...] = (xf * lax.rsqrt(ms + NORM_EPS) * gain_ref[...]).astype(BF16)
        xv = xn_ref[...]
    else:
        xv = x_ref[0]
    contract = (((1,), (1,)), ((), ())) if w_transposed else (((1,), (0,)), ((), ()))
    acc = lax.dot_general(xv, w_ref[...].astype(BF16), contract, preferred_element_type=F32)
    if gate:
        acc = jax.nn.sigmoid(gate_ref[0].astype(F32)) * acc
    if add:
        acc = acc + add_ref[0].astype(F32)
    if rope:
        tab = tab_ref[0]
        for hh in range(acc.shape[1] // A_QW):
            c0 = hh * A_QW
            r = acc[:, c0 + LANES:c0 + A_QW] * tab
            r = r + pltpu.roll(r, A_ROPE, axis=1)
            o_ref[0, :, c0:c0 + LANES] = (acc[:, c0:c0 + LANES] * scale).astype(o_ref.dtype)
            o_ref[0, :, c0 + LANES:c0 + A_QW] = (r * scale).astype(o_ref.dtype)
    else:
        o_ref[0] = acc.astype(o_ref.dtype)


def _matmul(x, w, *, rows, tm, tn, n_out, out_dtype, x_col=0, k=None, w_col=0, wt_row0=None,
            rms_gain=None, gate=None, gate_col=0, add=None, rope_tab=None, scale=1.0, name="mm"):
    B = x.shape[0]
    k = x.shape[2] if k is None else k
    assert rows % tm == 0 and n_out % tn == 0
    if wt_row0 is None:
        assert w.shape[0] == k
        w_spec = pl.BlockSpec((k, tn), lambda b, i, j: (0, j + w_col))
    else:
        assert w.shape[1] == k and wt_row0 % 8 == 0
        w_spec = pl.BlockSpec((pl.Element(tn), pl.Element(k)),
                              lambda b, i, j: (pl.multiple_of(wt_row0 + j * tn, 8), 0))
    in_specs = [pl.BlockSpec((1, tm, k), lambda b, i, j: (b, i, x_col)), w_spec]
    args = [x, w]
    scratch = []
    if rms_gain is not None:
        in_specs.append(pl.BlockSpec((1, k), lambda b, i, j: (0, 0)))
        args.append(rms_gain.reshape(1, k))
        scratch.append(pltpu.VMEM((tm, k), BF16))
    if gate is not None:
        in_specs.append(pl.BlockSpec((1, tm, tn), lambda b, i, j: (b, i, j + gate_col)))
        args.append(gate)
    if add is not None:
        in_specs.append(pl.BlockSpec((1, tm, tn), lambda b, i, j: (b, i, j)))
        args.append(add)
    if rope_tab is not None:
        in_specs.append(pl.BlockSpec((1, tm, LANES), lambda b, i, j: (0, i, 0)))
        args.append(rope_tab)
    kern = functools.partial(_mm_kernel, rms=rms_gain is not None, gate=gate is not None, add=add is not None,
                             rope=rope_tab is not None, scale=scale, w_transposed=wt_row0 is not None)
    return pl.pallas_call(
        kern,
        grid=(B, rows // tm, n_out // tn),
        in_specs=in_specs,
        out_specs=pl.BlockSpec((1, tm, tn), lambda b, i, j: (b, i, j)),
        out_shape=jax.ShapeDtypeStruct((B, rows, n_out), out_dtype),
        scratch_shapes=scratch,
        compiler_params=_params(("arbitrary", "arbitrary", "arbitrary")),
        name=name,
    )(*args)


def _small_kernel(x_ref, wg_ref, wk_ref, bias_ref, cos_ref, sin_ref, zg_ref, kr_ref):
    xv = x_ref[0]
    w = jnp.concatenate([wg_ref[...], wk_ref[...]], axis=0)
    w_hi = w.astype(BF16)
    w_lo = (w - w_hi.astype(F32)).astype(BF16)
    nt = (((1,), (1,)), ((), ()))
    acc = (lax.dot_general(xv, w_hi, nt, preferred_element_type=F32)
           + lax.dot_general(xv, w_lo, nt, preferred_element_type=F32))
    zg_ref[0] = acc[:, :LANES] + bias_ref[...]
    kr = acc[:, LANES:]
    lane = lax.broadcasted_iota(jnp.int32, kr.shape, 1)
    partner = jnp.where((lane & 1) == 0, -pltpu.roll(kr, LANES - 1, axis=1), pltpu.roll(kr, 1, axis=1))
    kr_ref[0] = (kr * cos_ref[0] + partner * sin_ref[0]).astype(kr_ref.dtype)


def _small_proj(h_all, w_in_t, bias, cos_k, sin_k, tm):
    B = h_all.shape[0]
    blk = lambda c: pl.BlockSpec((1, tm, c), lambda b, i: (b, i, 0))
    tab = pl.BlockSpec((1, tm, LANES), lambda b, i: (0, i, 0))
    w_rows = lambda r0: pl.BlockSpec((pl.Element(LANES), pl.Element(D_MODEL)), lambda b, i: (r0, 0))
    return pl.pallas_call(
        _small_kernel,
        grid=(B, ROWS_ALL // tm),
        in_specs=[blk(D_MODEL), w_rows(OFF_G), w_rows(OFF_KR),
                  pl.BlockSpec((1, LANES), lambda b, i: (0, 0)),
                  tab, tab],
        out_specs=[blk(LANES), blk(LANES)],
        out_shape=[jax.ShapeDtypeStruct((B, ROWS_ALL, LANES), F32),
                   jax.ShapeDtypeStruct((B, ROWS_ALL, LANES), BF16)],
        compiler_params=_params(("arbitrary", "arbitrary")),
        name="small_proj",
    )(h_all, w_in_t, w_in_t, bias, cos_k, sin_k)


def _log_sigmoid(x):
    return jnp.minimum(x, 0.0) - jnp.log1p(jnp.exp(-jnp.abs(x)))


MLSTM_HP = 4


def _mlstm_dir(q_ref, k_ref, v_ref, zg_ref, o_ref, c_ref, n_ref, m_ref, *, reverse, head0):
    L = M_CHUNK
    zg = zg_ref[0]
    lane = lax.broadcasted_iota(jnp.int32, (L, LANES), 1)
    sub = lax.broadcasted_iota(jnp.int32, (LANES, L), 0)
    row_id = lax.broadcasted_iota(jnp.int32, (L, L), 0)
    col_id = lax.broadcasted_iota(jnp.int32, (L, L), 1)
    seen = (col_id >= row_id) if reverse else (col_id <= row_id)

    lf = _log_sigmoid(zg)
    tri = jnp.where(seen, 1.0, 0.0).astype(BF16)
    hi, mid, lo = _split3(lf)
    bcum = (jnp.dot(tri, hi, preferred_element_type=F32)
            + jnp.dot(tri, mid, preferred_element_type=F32)
            + jnp.dot(tri, lo, preferred_element_type=F32))

    bcum_t = bcum.T
    zg_t = zg.T

    def pick_col(a, c):
        return jnp.sum(jnp.where(lane == c, a, 0.0), axis=1, keepdims=True)

    def pick_row(a_t, c):
        return jnp.sum(jnp.where(sub == c, a_t, 0.0), axis=0, keepdims=True)

    for hh in range(MLSTM_HP):
        col_i = head0 + hh + (2 * M_HEADS if reverse else 0)
        col_f = col_i + M_HEADS
        q = q_ref[0, :, hh * M_DQK:(hh + 1) * M_DQK]
        k = k_ref[0, :, hh * M_DQK:(hh + 1) * M_DQK]
        v = v_ref[0, :, hh * M_DV:(hh + 1) * M_DV]
        c_prev = c_ref[hh]
        n_prev = n_ref[hh]
        m_prev = m_ref[hh]

        b_col = pick_col(bcum, col_f)
        i_col = pick_col(zg, col_i)
        b_row = pick_row(bcum_t, col_f)
        i_row = pick_row(zg_t, col_i)
        b_last = b_col[0:1, :] if reverse else b_col[L - 1:L, :]

        d = jnp.where(seen, b_col - b_row + i_row, -jnp.inf)
        inter = b_col + m_prev
        m_t = jnp.maximum(inter, jnp.max(d, axis=1, keepdims=True))
        w = jnp.exp(d - m_t) * (M_DQK ** -0.5)
        scale = jnp.exp(inter - m_t)
        s = lax.dot_general(q, k, (((1,), (1,)), ((), ())), preferred_element_type=F32) * w
        cq = lax.dot_general(q, c_prev.astype(BF16), (((1,), (1,)), ((), ())),
                             preferred_element_type=F32)
        num = jnp.dot(s.astype(BF16), v, preferred_element_type=F32) + scale * cq
        nq = jnp.sum(q.astype(F32) * n_prev, axis=1, keepdims=True)
        den = jnp.sum(s, axis=1, keepdims=True) + scale * nq
        h_out = num / jnp.maximum(jnp.abs(den), jnp.exp(-m_t))
        o_ref[0, :, hh * M_DV:(hh + 1) * M_DV] = h_out.astype(o_ref.dtype)

        g = b_last - b_col + i_col
        m_new = jnp.maximum(b_last + m_prev, jnp.max(g, axis=0, keepdims=True))
        wk = jnp.exp(g - m_new) * (M_DQK ** -0.5)
        dec = jnp.exp(b_last + m_prev - m_new)
        vw = (v.astype(F32) * wk).astype(BF16)
        upd = lax.dot_general(vw, k, (((0,), (0,)), ((), ())), preferred_element_type=F32)
        c_ref[hh] = dec * c_prev + upd
        n_ref[hh] = dec * n_prev + jnp.sum(k.astype(F32) * wk, axis=0, keepdims=True)
        m_ref[hh] = m_new


def _mlstm_kernel(qf, kf, vf, gf, qb, kb, vb, gb, of, ob, cf, nf, mf, cb, nb, mb):
    head0 = pl.program_id(1) * MLSTM_HP

    @pl.when(pl.program_id(2) == 0)
    def _():
        for r in (cf, nf, mf, cb, nb, mb):
            r[...] = jnp.zeros_like(r)

    _mlstm_dir(qf, kf, vf, gf, of, cf, nf, mf, reverse=False, head0=head0)
    _mlstm_dir(qb, kb, vb, gb, ob, cb, nb, mb, reverse=True, head0=head0)


def _mlstm(z_a, zg):
    B = z_a.shape[0]
    L = M_CHUNK
    HP = MLSTM_HP
    nlat = SEQ // L
    assert CTX_LEN == L and M_HEADS % HP == 0
    k_blk0 = M_QK_W // (HP * M_DQK)
    v_blk0 = (2 * M_QK_W) // (HP * M_DV)

    def cf(j):
        return jnp.where(j == 0, nlat, j - 1)

    def cb(j):
        return jnp.where(j == 0, nlat, nlat - j)

    def specs(ch):
        return [pl.BlockSpec((1, L, HP * M_DQK), lambda b, h, j: (b, ch(j), h)),
                pl.BlockSpec((1, L, HP * M_DQK), lambda b, h, j: (b, ch(j), k_blk0 + h)),
                pl.BlockSpec((1, L, HP * M_DV), lambda b, h, j: (b, ch(j), v_blk0 + h)),
                pl.BlockSpec((1, L, LANES), lambda b, h, j: (b, ch(j), 0))]

    out_sds = jax.ShapeDtypeStruct((B, SEQ, M_V_W), BF16)
    state = [pltpu.VMEM((HP, M_DV, M_DQK), F32), pltpu.VMEM((HP, 1, M_DQK), F32),
             pltpu.VMEM((HP, 1, 1), F32)]
    return pl.pallas_call(
        _mlstm_kernel,
        grid=(B, M_HEADS // HP, nlat + 1),
        in_specs=specs(cf) + specs(cb),
        out_specs=[pl.BlockSpec((1, L, HP * M_DV), lambda b, h, j: (b, jnp.maximum(j - 1, 0), h)),
                   pl.BlockSpec((1, L, HP * M_DV), lambda b, h, j: (b, jnp.minimum(nlat - j, nlat - 1), h))],
        out_shape=[out_sds, out_sds],
        scratch_shapes=state + state,
        compiler_params=_params(("arbitrary", "arbitrary", "arbitrary")),
        name="mlstm_scan",
    )(z_a, z_a, z_a, zg, z_a, z_a, z_a, zg)


def _mlstm_out_kernel(hf_ref, hb_ref, zo_ref, g_ref, o_ref):
    for h in range(M_HEADS):
        sl = slice(h * M_DV, (h + 1) * M_DV)
        hs = hf_ref[0, :, sl].astype(F32) + hb_ref[0, :, sl].astype(F32)
        ms = jnp.mean(hs * hs, axis=-1, keepdims=True)
        hn = hs * lax.rsqrt(ms + NORM_EPS) * g_ref[:, sl]
        o_ref[0, :, sl] = (hn * jax.nn.sigmoid(zo_ref[0, :, sl].astype(F32))).astype(o_ref.dtype)


def _mlstm_out(h_f, h_b, z_o, gain, tm=256):
    B = h_f.shape[0]
    blk = pl.BlockSpec((1, tm, M_V_W), lambda b, i: (b, i, 0))
    return pl.pallas_call(
        _mlstm_out_kernel,
        grid=(B, SEQ // tm),
        in_specs=[blk, blk, blk, pl.BlockSpec((1, M_V_W), lambda b, i: (0, 0))],
        out_specs=blk,
        out_shape=jax.ShapeDtypeStruct((B, SEQ, M_V_W), BF16),
        compiler_params=_params(("arbitrary", "arbitrary")),
        name="mlstm_out",
    )(h_f, h_b, z_o, gain.reshape(1, M_V_W))


ATTN_KEY_BLOCK = 768


ATTN_HP = 2


def _attn_kernel(q_ref, kv_ref, kr_ref, o_ref):
    kvw = A_NOPE + A_DV
    m = [None] * ATTN_HP
    l = [None] * ATTN_HP
    acc = [None] * ATTN_HP
    for j in range(ROWS_ALL // ATTN_KEY_BLOCK):
        ks = slice(j * ATTN_KEY_BLOCK, (j + 1) * ATTN_KEY_BLOCK)
        kr = kr_ref[0, ks, :]
        for hh in range(ATTN_HP):
            q = q_ref[0, :, hh * A_QW:(hh + 1) * A_QW]
            kcat = jnp.concatenate([kv_ref[0, ks, hh * kvw:hh * kvw + A_NOPE], kr], axis=1)
            s = lax.dot_general(q, kcat, (((1,), (1,)), ((), ())), preferred_element_type=F32)
            m_j = jnp.max(s, axis=1, keepdims=True)
            m_new = m_j if j == 0 else jnp.maximum(m[hh], m_j)
            p = jnp.exp(s - m_new)
            l_j = jnp.sum(p, axis=1, keepdims=True)
            pv = jnp.dot(p.astype(BF16), kv_ref[0, ks, hh * kvw + A_NOPE:(hh + 1) * kvw],
                         preferred_element_type=F32)
            if j == 0:
                l[hh], acc[hh] = l_j, pv
            else:
                alpha = jnp.exp(m[hh] - m_new)
                l[hh] = alpha * l[hh] + l_j
                acc[hh] = alpha * acc[hh] + pv
            m[hh] = m_new
    for hh in range(ATTN_HP):
        o_ref[0, :, hh * A_DV:(hh + 1) * A_DV] = (acc[hh] / l[hh]).astype(o_ref.dtype)


def _attention(q, kv, krz, tq=1024):
    B = q.shape[0]
    HP = ATTN_HP
    return pl.pallas_call(
        _attn_kernel,
        grid=(B, A_HEADS // HP, SEQ // tq),
        in_specs=[pl.BlockSpec((1, tq, HP * A_QW), lambda b, h, i: (b, i, h)),
                  pl.BlockSpec((1, ROWS_ALL, HP * (A_NOPE + A_DV)), lambda b, h, i: (b, 0, h)),
                  pl.BlockSpec((1, ROWS_ALL, LANES), lambda b, h, i: (b, 0, 0))],
        out_specs=pl.BlockSpec((1, tq, HP * A_DV), lambda b, h, i: (b, i, h)),
        out_shape=jax.ShapeDtypeStruct((B, SEQ, A_HEADS * A_DV), BF16),
        compiler_params=_params(("arbitrary", "arbitrary", "arbitrary")),
        name="latent_attn",
    )(q, kv, krz)


def _post1_kernel(x_ref, y_ref, gpost_ref, gpre_ref, g1_ref, sc_ref, sh_ref, rwh_ref, rwl_ref, rb_ref,
                  x1_ref, h2_ref, eidx_ref, gates_ref, rank_ref, cnt_ref, carry_ref):
    first = jnp.logical_and(pl.program_id(0) == 0, pl.program_id(1) == 0)

    @pl.when(first)
    def _():
        carry_ref[...] = jnp.zeros_like(carry_ref)

    y = y_ref[0].astype(F32)
    yn = y * lax.rsqrt(jnp.mean(y * y, axis=-1, keepdims=True) + NORM_EPS) * gpost_ref[...]
    x1 = x_ref[0] + g1_ref[0, 0] * yn
    x1_ref[0] = x1
    hn = x1 * lax.rsqrt(jnp.mean(x1 * x1, axis=-1, keepdims=True) + NORM_EPS) * gpre_ref[...]
    h2 = hn * (1.0 + sc_ref[0, 0]) + sh_ref[0, 0]
    h2_ref[0] = h2

    h_hi = h2.astype(BF16)
    h_lo = (h2 - h_hi.astype(F32)).astype(BF16)
    logits = (jnp.dot(h_hi, rwh_ref[...], preferred_element_type=F32)
              + jnp.dot(h_lo, rwh_ref[...], preferred_element_type=F32)
              + jnp.dot(h_hi, rwl_ref[...], preferred_element_type=F32)) + rb_ref[...]
    tm = logits.shape[0]
    lane = lax.broadcasted_iota(jnp.int32, (tm, LANES), 1)
    lane_f = lane.astype(F32)
    work = jnp.where(lane < N_EXPERTS, logits, -jnp.inf)
    sel = jnp.zeros((tm, LANES), F32)
    tops, hots = [], []
    for _ in range(TOP_K):
        mx = jnp.max(work, axis=1, keepdims=True)
        idx = jnp.min(jnp.where(work == mx, lane_f, float(LANES)), axis=1, keepdims=True)
        hot = lane_f == idx
        tops.append(mx)
        hots.append(hot)
        sel = jnp.where(hot, 1.0, sel)
        work = jnp.where(hot, -jnp.inf, work)
    es = [jnp.exp(t - tops[0]) for t in tops]
    tot = es[0] + es[1] + es[2] + es[3]

    r_id = lax.broadcasted_iota(jnp.int32, (tm, tm), 0)
    c_id = lax.broadcasted_iota(jnp.int32, (tm, tm), 1)
    strict = jnp.where(c_id < r_id, 1.0, 0.0).astype(BF16)
    rank_all = jnp.dot(strict, sel.astype(BF16), preferred_element_type=F32) + carry_ref[...]
    carry_ref[...] = carry_ref[...] + jnp.sum(sel, axis=0, keepdims=True)
    cnt_ref[...] = carry_ref[...]

    eidx = jnp.zeros((tm, LANES), jnp.int32)
    gates = jnp.zeros((tm, LANES), F32)
    rank = jnp.zeros((tm, LANES), jnp.int32)
    for kk in range(TOP_K):
        e_k = jnp.sum(jnp.where(hots[kk], lane_f, 0.0), axis=1, keepdims=True)
        r_k = jnp.sum(jnp.where(hots[kk], rank_all, 0.0), axis=1, keepdims=True)
        eidx = jnp.where(lane == kk, e_k.astype(jnp.int32), eidx)
        gates = jnp.where(lane == kk, es[kk] / tot, gates)
        rank = jnp.where(lane == kk, r_k.astype(jnp.int32), rank)
    eidx_ref[0] = eidx
    gates_ref[0] = gates
    rank_ref[0] = rank


def _post1(x, y, gpost, gpre, mod6, rw_hi, rw_lo, rb, tm=256):
    B = x.shape[0]
    blk = pl.BlockSpec((1, tm, D_MODEL), lambda b, i: (b, i, 0))
    vec = pl.BlockSpec((1, D_MODEL), lambda b, i: (0, 0))
    lan = pl.BlockSpec((1, tm, LANES), lambda b, i: (b, i, 0))
    rw = pl.BlockSpec((D_MODEL, LANES), lambda b, i: (0, 0))

    def mod_spec(k):
        return pl.BlockSpec((1, 1, 1, D_MODEL), lambda b, i: (k, b, 0, 0))

    lan_i = jax.ShapeDtypeStruct((B, SEQ, LANES), jnp.int32)
    return pl.pallas_call(
        _post1_kernel,
        grid=(B, SEQ // tm),
        in_specs=[blk, blk, vec, vec, mod_spec(2), mod_spec(4), mod_spec(3), rw, rw,
                  pl.BlockSpec((1, LANES), lambda b, i: (0, 0))],
        out_specs=[blk, blk, lan, lan, lan,
                   pl.BlockSpec((1, LANES), lambda b, i: (0, 0))],
        out_shape=[jax.ShapeDtypeStruct((B, SEQ, D_MODEL), F32),
                   jax.ShapeDtypeStruct((B, SEQ, D_MODEL), F32),
                   lan_i, jax.ShapeDtypeStruct((B, SEQ, LANES), F32), lan_i,
                   jax.ShapeDtypeStruct((1, LANES), F32)],
        scratch_shapes=[pltpu.VMEM((1, LANES), F32)],
        compiler_params=_params(("arbitrary", "arbitrary")),
        name="post_mix",
    )(x, y, gpost.reshape(1, D_MODEL), gpre.reshape(1, D_MODEL), mod6, mod6, mod6, rw_hi, rw_lo, rb)


DMA_ISSUE_UNROLL = 8


def _gather_kernel(idx_ref, tv_ref, src_ref, o_ref, land_ref, sem):
    i = pl.program_id(0)
    n = pl.num_programs(0)
    slot = i % 2

    def for_row_multiple(rows, fn):
        for m in range(MOE_ROW_STEP, MOE_TILE + 1, MOE_ROW_STEP):
            pl.when(jnp.logical_and(rows > m - MOE_ROW_STEP, rows <= m))(functools.partial(fn, m))

    def issue(t, sl):
        base = t * MOE_TILE

        def start(j, c):
            for prio in range(2):
                r = 2 * j + prio
                pltpu.make_async_copy(src_ref.at[pl.ds(idx_ref[base + r], 1)], land_ref.at[sl, pl.ds(r, 1)],
                                      sem.at[sl]).start(priority=prio)
            return c

        def start_rows(m):
            lax.fori_loop(0, m // 2, start, 0, unroll=DMA_ISSUE_UNROLL // 2)

        for_row_multiple(tv_ref[t], start_rows)

    @pl.when(i == 0)
    def _():
        land_ref[...] = jnp.zeros_like(land_ref)
        issue(0, 0)

    @pl.when(i + 1 < n)
    def _():
        issue(jnp.minimum(i + 1, n - 1), 1 - slot)

    def wait(m):
        pltpu.make_async_copy(src_ref.at[pl.ds(0, m)], land_ref.at[slot, pl.ds(0, m)], sem.at[slot]).wait()

    for_row_multiple(tv_ref[i], wait)

    @pl.when(tv_ref[i] > 0)
    def _():
        o_ref[...] = land_ref[slot].astype(o_ref.dtype)

    @pl.when(tv_ref[i] == 0)
    def _():
        o_ref[...] = jnp.zeros_like(o_ref)


def _gather_rows(slot_tok, tile_v, src):
    n = slot_tok.shape[0]
    return pl.pallas_call(
        _gather_kernel,
        grid_spec=pltpu.PrefetchScalarGridSpec(
            num_scalar_prefetch=2,
            grid=(n // MOE_TILE,),
            in_specs=[pl.BlockSpec(memory_space=pl.ANY)],
            out_specs=pl.BlockSpec((MOE_TILE, D_MODEL), lambda i, idx, tv: (i, 0)),
            scratch_shapes=[pltpu.VMEM((2, MOE_TILE, D_MODEL), F32),
                            pltpu.SemaphoreType.DMA((2,))]),
        out_shape=jax.ShapeDtypeStruct((n, D_MODEL), BF16),
        compiler_params=_params(("arbitrary",)),
        name="moe_gather",
    )(slot_tok, tile_v, src)


MOE_ROW_STEP = 128


def _for_row_count(rows, run, o_ref):
    for m in range(MOE_ROW_STEP, MOE_TILE + 1, MOE_ROW_STEP):
        @pl.when(jnp.logical_and(rows > m - MOE_ROW_STEP, rows <= m))
        def _(m=m):
            run(m)
            if m < MOE_TILE:
                o_ref[m:, :] = jnp.zeros((MOE_TILE - m, o_ref.shape[1]), o_ref.dtype)

    @pl.when(rows == 0)
    def _():
        o_ref[...] = jnp.zeros_like(o_ref)


def _stream_expert_weights(te_ref, fr_ref, ne_ref, lg_ref, copies, cast):
    p = pl.program_id(0)
    i = pl.program_id(1)

    @pl.when(fr_ref[i] == 1)
    def _():
        @pl.when(jnp.logical_and(p == 0, i == 0))
        def _():
            for c in copies(te_ref[i], p):
                c.start()

        for c in copies(te_ref[i], p):
            c.wait()
        cast()
        is_last_group = lg_ref[i] == 1

        @pl.when(jnp.logical_not(jnp.logical_and(is_last_group, p == pl.num_programs(0) - 1)))
        def _():
            for c in copies(ne_ref[i], p + lg_ref[i]):
                c.start()


def _gmm1_kernel(te_ref, tv_ref, fr_ref, ne_ref, lg_ref, x_ref, bg_ref, bu_ref, w_hbm, o_ref,
                 land_ref, wb_ref, sem):
    i = pl.program_id(1)
    tn = o_ref.shape[1]

    def copies(e, p):
        c0 = pl.multiple_of(p * tn, tn)
        return [pltpu.make_async_copy(w_hbm.at[e, :, pl.ds(c0 + off, tn)], land_ref.at[s], sem.at[s])
                for s, off in enumerate((0, D_EXPERT))]

    def cast():
        for s in range(2):
            wb_ref[s] = land_ref[s].astype(BF16)

    _stream_expert_weights(te_ref, fr_ref, ne_ref, lg_ref, copies, cast)

    def run(m):
        xv = x_ref[:m, :]
        glu = jnp.dot(xv, wb_ref[0], preferred_element_type=F32) + bg_ref[0]
        lin = jnp.dot(xv, wb_ref[1], preferred_element_type=F32) + bu_ref[0]
        glu = jnp.minimum(glu, SWIGLU_LIMIT)
        lin = jnp.clip(lin, -SWIGLU_LIMIT, SWIGLU_LIMIT)
        o_ref[:m, :] = (glu * jax.nn.sigmoid(SWIGLU_ALPHA * glu) * (lin + 1.0)).astype(o_ref.dtype)

    _for_row_count(tv_ref[i], run, o_ref)


def _gmm1(tiles, xs, w_gu, b_gu, tn=512):
    n_slots = xs.shape[0]
    n_tiles = n_slots // MOE_TILE
    nf = D_EXPERT // tn
    b3 = b_gu.reshape(N_EXPERTS, 1, 2 * D_EXPERT)
    return pl.pallas_call(
        _gmm1_kernel,
        grid_spec=pltpu.PrefetchScalarGridSpec(
            num_scalar_prefetch=5,
            grid=(nf, n_tiles),
            in_specs=[pl.BlockSpec((MOE_TILE, D_MODEL), lambda n, i, te, *_: (i, 0)),
                      pl.BlockSpec((1, 1, tn), lambda n, i, te, *_: (te[i], 0, n)),
                      pl.BlockSpec((1, 1, tn), lambda n, i, te, *_: (te[i], 0, nf + n)),
                      pl.BlockSpec(memory_space=pl.ANY)],
            out_specs=pl.BlockSpec((MOE_TILE, tn), lambda n, i, te, *_: (i, n)),
            scratch_shapes=[pltpu.VMEM((2, D_MODEL, tn), F32), pltpu.VMEM((2, D_MODEL, tn), BF16),
                            pltpu.SemaphoreType.DMA((2,))]),
        out_shape=jax.ShapeDtypeStruct((n_slots, D_EXPERT), BF16),
        compiler_params=_params(("arbitrary", "arbitrary")),
        name="moe_gate_up",
    )(*tiles, xs, b3, b3, w_gu)


def _gmm2_kernel(te_ref, tv_ref, fr_ref, ne_ref, lg_ref, a_ref, b_ref, w_hbm, o_ref, land_ref, wb_ref, sem):
    i = pl.program_id(1)
    tn = wb_ref.shape[1]

    def copies(e, p):
        return [pltpu.make_async_copy(w_hbm.at[e, :, pl.ds(pl.multiple_of(p * tn, tn), tn)], land_ref, sem)]

    def cast():
        wb_ref[...] = land_ref[...].astype(BF16)

    _stream_expert_weights(te_ref, fr_ref, ne_ref, lg_ref, copies, cast)

    def run(m):
        o_ref[:m, :] = jnp.dot(a_ref[:m, :], wb_ref[...], preferred_element_type=F32) + b_ref[0]

    _for_row_count(tv_ref[i], run, o_ref)


def _gmm2(tiles, act, w_down, b_down, tn):
    n_slots = act.shape[0]
    n_tiles = n_slots // MOE_TILE
    nn = D_MODEL // tn
    return pl.pallas_call(
        _gmm2_kernel,
        grid_spec=pltpu.PrefetchScalarGridSpec(
            num_scalar_prefetch=5,
            grid=(nn, n_tiles),
            in_specs=[pl.BlockSpec((MOE_TILE, D_EXPERT), lambda n, i, te, *_: (i, 0)),
                      pl.BlockSpec((1, 1, tn), lambda n, i, te, *_: (te[i], 0, n)),
                      pl.BlockSpec(memory_space=pl.ANY)],
            out_specs=pl.BlockSpec((MOE_TILE, tn), lambda n, i, te, *_: (i, n)),
            scratch_shapes=[pltpu.VMEM((D_EXPERT, tn), F32), pltpu.VMEM((D_EXPERT, tn), BF16),
                            pltpu.SemaphoreType.DMA]),
        out_shape=jax.ShapeDtypeStruct((n_slots, D_MODEL), F32),
        compiler_params=_params(("arbitrary", "arbitrary")),
        name="moe_down",
    )(*tiles, act, b_down.reshape(N_EXPERTS, 1, D_MODEL), w_down)


COMBINE_TM = 128
GMM2_TN = 2048


def _combine_kernel(dest_ref, x1_ref, gates_ref, gpost_ref, g2_ref, ys_ref, o_ref, buf_ref, sem):
    tm = COMBINE_TM
    n_rows = tm * TOP_K
    n_steps = pl.num_programs(0) * pl.num_programs(1)
    step = pl.program_id(0) * pl.num_programs(1) + pl.program_id(1)
    slot = step % 2

    def issue(st, sl):
        base = st * n_rows

        def start(j, c):
            for prio in range(2):
                r = 2 * j + prio
                pltpu.make_async_copy(ys_ref.at[pl.ds(dest_ref[base + r], 1)],
                                      buf_ref.at[sl, pl.ds(r, 1)], sem.at[sl]).start(priority=prio)
            return c

        lax.fori_loop(0, n_rows // 2, start, 0, unroll=DMA_ISSUE_UNROLL // 2)

    @pl.when(step == 0)
    def _():
        issue(0, 0)

    @pl.when(step + 1 < n_steps)
    def _():
        issue(step + 1, 1 - slot)

    pltpu.make_async_copy(ys_ref.at[pl.ds(0, n_rows)], buf_ref.at[slot], sem.at[slot]).wait()

    gates = gates_ref[0]
    lane = lax.broadcasted_iota(jnp.int32, gates.shape, 1)
    f = None
    for kk in range(TOP_K):
        gk = jnp.sum(jnp.where(lane == kk, gates, 0.0), axis=1, keepdims=True)
        term = gk * buf_ref[slot, pl.ds(kk * tm, tm), :]
        f = term if f is None else f + term
    fn = f * lax.rsqrt(jnp.mean(f * f, axis=-1, keepdims=True) + NORM_EPS) * gpost_ref[...]
    o_ref[0] = x1_ref[0] + g2_ref[0, 0] * fn


def _combine(dest, x1, gates, gpost, mod6, ys):
    B = x1.shape[0]
    tm = COMBINE_TM
    blk = pl.BlockSpec((1, tm, D_MODEL), lambda b, i, d: (b, i, 0))
    return pl.pallas_call(
        _combine_kernel,
        grid_spec=pltpu.PrefetchScalarGridSpec(
            num_scalar_prefetch=1,
            grid=(B, SEQ // tm),
            in_specs=[blk,
                      pl.BlockSpec((1, tm, LANES), lambda b, i, d: (b, i, 0)),
                      pl.BlockSpec((1, D_MODEL), lambda b, i, d: (0, 0)),
                      pl.BlockSpec((1, 1, 1, D_MODEL), lambda b, i, d: (5, b, 0, 0)),
                      pl.BlockSpec(memory_space=pl.ANY)],
            out_specs=blk,
            scratch_shapes=[pltpu.VMEM((2, tm * TOP_K, D_MODEL), F32),
                            pltpu.SemaphoreType.DMA((2,))]),
        out_shape=jax.ShapeDtypeStruct((B, SEQ, D_MODEL), F32),
        compiler_params=_params(("arbitrary", "arbitrary")),
        name="moe_combine",
    )(dest, x1, gates, gpost.reshape(1, D_MODEL), mod6, ys)


def _rope_tables():
    rows = SEQ // GRID_W
    r, col = jnp.meshgrid(jnp.arange(rows, dtype=F32), jnp.arange(GRID_W, dtype=F32), indexing="ij")
    n_freq = A_ROPE // 4
    inv = ROPE_BASE ** (-jnp.arange(n_freq, dtype=F32) / n_freq)
    ang = jnp.concatenate([r.reshape(-1, 1) * inv, col.reshape(-1, 1) * inv], axis=-1)
    cos = jnp.concatenate([jnp.repeat(jnp.cos(ang), 2, axis=-1), jnp.ones((CTX_LEN, A_ROPE), F32)], axis=0)
    sin = jnp.concatenate([jnp.repeat(jnp.sin(ang), 2, axis=-1), jnp.zeros((CTX_LEN, A_ROPE), F32)], axis=0)
    pad = ((0, 0), (0, LANES - A_ROPE))
    return jnp.concatenate([cos, sin], axis=-1)[None], jnp.pad(cos, pad)[None], jnp.pad(sin, pad)[None]


def _rot_partner(w):
    wp = w.reshape(*w.shape[:-1], A_ROPE // 2, 2)
    return jnp.stack([-wp[..., 1], wp[..., 0]], axis=-1).reshape(w.shape)


def _layer(x, ctx, mod6, tabs, norm_pre_mix, norm_post_mix, norm_pre_ffn, norm_post_ffn, w_in, b_gates,
           m_out_norm, q_norm, kv_norm, w_uq, w_ukv, w_branch_a, w_branch_b, w_out, router_w, router_b,
           w_gu, b_gu, w_down, b_down):
    B = x.shape[0]

    tab, cos_k, sin_k = tabs
    w_in_t = w_in.T
    gate_bias = jnp.concatenate([b_gates, jnp.zeros((LANES - N_GATE_COLS,), F32)]).reshape(1, LANES)
    wq = w_uq.reshape(Q_LORA, A_HEADS, A_NOPE + A_ROPE)
    wq_r = wq[..., A_NOPE:]
    w_uq2 = jnp.concatenate([wq[..., :A_NOPE], wq_r, _rot_partner(wq_r)], axis=-1)
    w_uq2 = w_uq2.reshape(Q_LORA, A_HEADS * A_QW).astype(BF16)

    h_all = _prenorm(x, ctx, norm_pre_mix, mod6)

    off_o = 2 * M_QK_W + M_V_W
    z_a = _matmul(h_all, w_in_t, rows=ROWS_ALL, tm=1152, tn=512, n_out=off_o, out_dtype=BF16, wt_row0=OFF_Q,
                  name="in_proj_a")
    z_o = _matmul(h_all, w_in_t, rows=SEQ, tm=1024, tn=512, n_out=M_V_W, out_dtype=BF16, wt_row0=off_o,
                  name="in_proj_o")
    z_b = _matmul(h_all, w_in_t, rows=ROWS_ALL, tm=1152, tn=512, n_out=Q_LORA + KV_LORA, out_dtype=BF16,
                  wt_row0=OFF_CQ, name="in_proj_b")
    z_d = _matmul(h_all, w_in_t, rows=SEQ, tm=1024, tn=512, n_out=2 * D_MODEL, out_dtype=BF16, wt_row0=OFF_GA,
                  name="in_proj_d")
    zg, krz = _small_proj(h_all, w_in_t, gate_bias, cos_k, sin_k, tm=1152)

    h_f, h_b = _mlstm(z_a, zg)
    ya = _mlstm_out(h_f, h_b, z_o, m_out_norm)

    q = _matmul(z_b, w_uq2, rows=SEQ, tm=1024, tn=1024, n_out=A_HEADS * A_QW, out_dtype=BF16, k=Q_LORA,
                rms_gain=q_norm, rope_tab=tab, scale=A_SCALE, name="q_up")
    kv = _matmul(z_b, w_ukv, rows=ROWS_ALL, tm=1152, tn=1024, n_out=A_HEADS * (A_NOPE + A_DV), out_dtype=BF16,
                 k=KV_LORA, x_col=Q_LORA // KV_LORA, rms_gain=kv_norm, name="kv_up")
    yb = _attention(q, kv, krz)

    ua = _matmul(ya, w_branch_a, rows=SEQ, tm=1024, tn=512, n_out=D_MODEL, out_dtype=BF16, gate=z_d,
                 name="branch_a")
    u = _matmul(yb, w_branch_b, rows=SEQ, tm=1024, tn=512, n_out=D_MODEL, out_dtype=BF16, gate=z_d,
                gate_col=D_MODEL // 512, add=ua, name="branch_b")
    y = _matmul(u, w_out, rows=SEQ, tm=1024, tn=512, n_out=D_MODEL, out_dtype=BF16, name="out_proj")

    rw = jnp.concatenate([router_w, jnp.zeros((D_MODEL, LANES - N_EXPERTS), F32)], axis=1)
    rw_hi = rw.astype(BF16)
    rw_lo = (rw - rw_hi.astype(F32)).astype(BF16)
    rb = jnp.concatenate([router_b, jnp.zeros((LANES - N_EXPERTS,), F32)]).reshape(1, LANES)
    x1, h2p, eidx, gates, rank, counts = _post1(x, y, norm_post_mix, norm_pre_ffn, mod6, rw_hi, rw_lo, rb)

    T = B * SEQ
    n_tiles = -(-(T * TOP_K + N_EXPERTS * (MOE_TILE - 1)) // MOE_TILE)
    n_slots = n_tiles * MOE_TILE
    cnt = counts[0, :N_EXPERTS].astype(jnp.int32)
    padded = (cnt + MOE_TILE - 1) // MOE_TILE * MOE_TILE
    g_end = jnp.cumsum(padded)
    g_start = g_end - padded
    e_flat = eidx.reshape(T, LANES)[:, :TOP_K]
    dest = g_start[e_flat] + rank.reshape(T, LANES)[:, :TOP_K]
    tok = jnp.broadcast_to(jnp.arange(T, dtype=jnp.int32)[:, None], (T, TOP_K))
    slot_tok = jnp.zeros((n_slots,), jnp.int32).at[dest.reshape(-1)].set(tok.reshape(-1))
    tile_lo = jnp.arange(n_tiles, dtype=jnp.int32) * MOE_TILE
    tile_e = jnp.minimum(jnp.sum((g_end[None, :] <= tile_lo[:, None]).astype(jnp.int32), axis=1), N_EXPERTS - 1)
    tile_v = jnp.clip(g_start[tile_e] + cnt[tile_e] - tile_lo, 0, MOE_TILE) * (tile_lo < g_end[-1])
    tile_v = tile_v.astype(jnp.int32)
    prev_e = jnp.concatenate([jnp.full((1,), -1, jnp.int32), tile_e[:-1]])
    tile_first = jnp.logical_and(tile_v > 0, tile_e != prev_e).astype(jnp.int32)
    after = g_end[tile_e] // MOE_TILE
    has_after = after * MOE_TILE < g_end[-1]
    tile_next_e = jnp.where(has_after, tile_e[jnp.minimum(after, n_tiles - 1)], tile_e[0]).astype(jnp.int32)
    tile_last = jnp.logical_not(has_after).astype(jnp.int32)
    tiles = (tile_e, tile_v, tile_first, tile_next_e, tile_last)
    dest_t = dest.reshape(T // COMBINE_TM, COMBINE_TM, TOP_K).transpose(0, 2, 1).reshape(-1)

    xs = _gather_rows(slot_tok, tile_v, h2p.reshape(T, D_MODEL))
    act = _gmm1(tiles, xs, w_gu, b_gu)
    ys = _gmm2(tiles, act, w_down, b_down, tn=GMM2_TN)
    return _combine(dest_t, x1, gates, norm_post_ffn, mod6, ys)


def kernel(x, c, ctx, c_ctx, w_ada, b_ada, norm_pre_mix, norm_post_mix, norm_pre_ffn, norm_post_ffn, w_in,
           b_gates, m_out_norm, q_norm, kv_norm, w_uq, w_ukv, w_branch_a, w_branch_b, w_out, router_w, router_b,
           w_gu, b_gu, w_down, b_down):
    B = x.shape[0]
    depth = w_ada.shape[0]
    assert depth == 1, "context-stream outputs between layers are not implemented"
    tabs = _rope_tables()
    cvec = jnp.concatenate([c, c_ctx[None], jnp.zeros((8 - B - 1, D_MODEL), F32)], axis=0)
    l = 0
    mod = _ada(cvec, w_ada[l], b_ada[l])
    mod6 = mod.reshape(8, N_MOD, 1, D_MODEL).transpose(1, 0, 2, 3)
    return _layer(x, ctx, mod6, tabs, norm_pre_mix[l], norm_post_mix[l], norm_pre_ffn[l], norm_post_ffn[l],
                  w_in[l], b_gates[l], m_out_norm[l], q_norm[l], kv_norm[l], w_uq[l], w_ukv[l], w_branch_a[l],
                  w_branch_b[l], w_out[l], router_w[l], router_b[l], w_gu[l], b_gu[l], w_down[l], b_down[l])
```

```python
import functools

import jax
import jax.numpy as jnp
import numpy as np
from jax import lax
from jax.experimental import pallas as pl
from jax.experimental.pallas import tpu as pltpu

F32 = jnp.float32
BF16 = jnp.bfloat16

D_MODEL = 4096
SEQ = 2048
GRID_W = 64
CTX_LEN = 256
ROWS_ALL = SEQ + CTX_LEN
NORM_EPS = 1e-6
N_MOD = 6

M_HEADS = 8
M_DQK = 256
M_DV = 512
M_QK_W = M_HEADS * M_DQK
M_V_W = M_HEADS * M_DV
N_GATE_COLS = 4 * M_HEADS
M_CHUNK = 256

A_HEADS = 32
A_NOPE = 128
A_ROPE = 64
A_DV = 128
Q_LORA = 1024
KV_LORA = 512
ROPE_BASE = 10000.0
A_SCALE = (A_NOPE + A_ROPE) ** -0.5
A_QW = 256

N_EXPERTS = 32
TOP_K = 4
D_EXPERT = 1536
SWIGLU_LIMIT = 7.0
SWIGLU_ALPHA = 1.702
MOE_TILE = 512

OFF_Q = 0
OFF_G = 2 * M_QK_W + 2 * M_V_W
OFF_CQ = OFF_G + N_GATE_COLS
OFF_CKV = OFF_CQ + Q_LORA
OFF_KR = OFF_CKV + KV_LORA
OFF_GA = OFF_KR + A_ROPE
OFF_GB = OFF_GA + D_MODEL

LANES = 128
V7X_VMEM_LIMIT = 56 * 1024 * 1024


def _params(sem, vmem=V7X_VMEM_LIMIT):
    return pltpu.CompilerParams(dimension_semantics=sem, vmem_limit_bytes=vmem)


def _split3(x):
    hi = x.astype(BF16)
    r1 = x - hi.astype(F32)
    mid = r1.astype(BF16)
    lo = (r1 - mid.astype(F32)).astype(BF16)
    return hi, mid, lo


def _ada_kernel(c_ref, w_ref, b_ref, o_ref):
    c = c_ref[...]
    cs = c * jax.nn.sigmoid(c)
    c_hi, c_mid, _ = _split3(cs)
    w_hi = w_ref[...].astype(BF16)
    acc = jnp.dot(c_hi, w_hi, preferred_element_type=F32)
    acc += jnp.dot(c_mid, w_hi, preferred_element_type=F32)
    o_ref[...] = acc + b_ref[...]


def _ada(cvec, w_ada, b_ada):
    n = w_ada.shape[1]
    tn = 1024
    return pl.pallas_call(
        _ada_kernel,
        grid=(n // tn,),
        in_specs=[pl.BlockSpec((8, D_MODEL), lambda j: (0, 0)),
                  pl.BlockSpec((D_MODEL, tn), lambda j: (0, j)),
                  pl.BlockSpec((1, tn), lambda j: (0, j))],
        out_specs=pl.BlockSpec((8, tn), lambda j: (0, j)),
        out_shape=jax.ShapeDtypeStruct((8, n), F32),
        compiler_params=_params(("arbitrary",)),
        name="ada_mod",
    )(cvec, w_ada, b_ada.reshape(1, n))


def _prenorm_kernel(x_ref, c_ref, g_ref, sc_ref, sh_ref, o_ref):
    j = pl.program_id(1)
    nlat = SEQ // CTX_LEN

    def run(src):
        xf = src[0]
        ms = jnp.mean(xf * xf, axis=-1, keepdims=True)
        y = xf * lax.rsqrt(ms + NORM_EPS) * g_ref[...]
        o_ref[0] = (y * (1.0 + sc_ref[0, 0]) + sh_ref[0, 0]).astype(o_ref.dtype)

    @pl.when(j < nlat)
    def _():
        run(x_ref)

    @pl.when(j == nlat)
    def _():
        run(c_ref)


def _prenorm(x, ctx, gain, mod6):
    B = x.shape[0]
    nlat = SEQ // CTX_LEN
    blk = (1, CTX_LEN, D_MODEL)

    def mod_spec(k):
        return pl.BlockSpec((1, 1, 1, D_MODEL),
                            lambda b, j: (k, jnp.where(j == nlat, B, b), 0, 0))

    return pl.pallas_call(
        _prenorm_kernel,
        grid=(B, nlat + 1),
        in_specs=[pl.BlockSpec(blk, lambda b, j: (b, jnp.minimum(j, nlat - 1), 0)),
                  pl.BlockSpec(blk, lambda b, j: (b, 0, 0)),
                  pl.BlockSpec((1, D_MODEL), lambda b, j: (0, 0)),
                  mod_spec(1), mod_spec(0)],
        out_specs=pl.BlockSpec(blk, lambda b, j: (b, j, 0)),
        out_shape=jax.ShapeDtypeStruct((B, ROWS_ALL, D_MODEL), BF16),
        compiler_params=_params(("arbitrary", "arbitrary")),
        name="prenorm",
    )(x, ctx, gain.reshape(1, D_MODEL), mod6, mod6)


def _mm_kernel(*refs, rms, gate, add, rope, scale, w_transposed):
    refs = list(refs)
    x_ref = refs.pop(0)
    w_ref = refs.pop(0)
    gain_ref = refs.pop(0) if rms else None
    gate_ref = refs.pop(0) if gate else None
    add_ref = refs.pop(0) if add else None
    tab_ref = refs.pop(0) if rope else None
    o_ref = refs.pop(0)
    xn_ref = refs.pop(0) if rms else None

    if rms:
        @pl.when(pl.program_id(2) == 0)
        def _():
            xf = x_ref[0].astype(F32)
            ms = jnp.mean(xf * xf, axis=-1, keepdims=True)
            xn_ref[...] = (xf * lax.rsqrt(ms + NORM_EPS) * gain_ref[...]).astype(BF16)
        xv = xn_ref[...]
    else:
        xv = x_ref[0]
    contract = (((1,), (1,)), ((), ())) if w_transposed else (((1,), (0,)), ((), ()))
    acc = lax.dot_general(xv, w_ref[...].astype(BF16), contract, preferred_element_type=F32)
    if gate:
        acc = jax.nn.sigmoid(gate_ref[0].astype(F32)) * acc
    if add:
        acc = acc + add_ref[0].astype(F32)
    if rope:
        tab = tab_ref[0]
        for hh in range(acc.shape[1] // A_QW):
            c0 = hh * A_QW
            r = acc[:, c0 + LANES:c0 + A_QW] * tab
            r = r + pltpu.roll(r, A_ROPE, axis=1)
            o_ref[0, :, c0:c0 + LANES] = (acc[:, c0:c0 + LANES] * scale).astype(o_ref.dtype)
            o_ref[0, :, c0 + LANES:c0 + A_QW] = (r * scale).astype(o_ref.dtype)
    else:
        o_ref[0] = acc.astype(o_ref.dtype)


def _matmul(x, w, *, rows, tm, tn, n_out, out_dtype, x_col=0, k=None, w_col=0, wt_row0=None,
            rms_gain=None, gate=None, gate_col=0, add=None, rope_tab=None, scale=1.0, name="mm"):
    B = x.shape[0]
    k = x.shape[2] if k is None else k
    assert rows % tm == 0 and n_out % tn == 0
    if wt_row0 is None:
        assert w.shape[0] == k
        w_spec = pl.BlockSpec((k, tn), lambda b, i, j: (0, j + w_col))
    else:
        assert w.shape[1] == k and wt_row0 % 8 == 0
        w_spec = pl.BlockSpec((pl.Element(tn), pl.Element(k)),
                              lambda b, i, j: (pl.multiple_of(wt_row0 + j * tn, 8), 0))
    in_specs = [pl.BlockSpec((1, tm, k), lambda b, i, j: (b, i, x_col)), w_spec]
    args = [x, w]
    scratch = []
    if rms_gain is not None:
        in_specs.append(pl.BlockSpec((1, k), lambda b, i, j: (0, 0)))
        args.append(rms_gain.reshape(1, k))
        scratch.append(pltpu.VMEM((tm, k), BF16))
    if gate is not None:
        in_specs.append(pl.BlockSpec((1, tm, tn), lambda b, i, j: (b, i, j + gate_col)))
        args.append(gate)
    if add is not None:
        in_specs.append(pl.BlockSpec((1, tm, tn), lambda b, i, j: (b, i, j)))
        args.append(add)
    if rope_tab is not None:
        in_specs.append(pl.BlockSpec((1, tm, LANES), lambda b, i, j: (0, i, 0)))
        args.append(rope_tab)
    kern = functools.partial(_mm_kernel, rms=rms_gain is not None, gate=gate is not None, add=add is not None,
                             rope=rope_tab is not None, scale=scale, w_transposed=wt_row0 is not None)
    return pl.pallas_call(
        kern,
        grid=(B, rows // tm, n_out // tn),
        in_specs=in_specs,
        out_specs=pl.BlockSpec((1, tm, tn), lambda b, i, j: (b, i, j)),
        out_shape=jax.ShapeDtypeStruct((B, rows, n_out), out_dtype),
        scratch_shapes=scratch,
        compiler_params=_params(("arbitrary", "arbitrary", "arbitrary")),
        name=name,
    )(*args)


def _small_kernel(x_ref, wg_ref, wk_ref, bias_ref, cos_ref, sin_ref, zg_ref, kr_ref):
    xv = x_ref[0]
    w = jnp.concatenate([wg_ref[...], wk_ref[...]], axis=0)
    w_hi = w.astype(BF16)
    w_lo = (w - w_hi.astype(F32)).astype(BF16)
    nt = (((1,), (1,)), ((), ()))
    acc = (lax.dot_general(xv, w_hi, nt, preferred_element_type=F32)
           + lax.dot_general(xv, w_lo, nt, preferred_element_type=F32))
    zg_ref[0] = acc[:, :LANES] + bias_ref[...]
    kr = acc[:, LANES:]
    lane = lax.broadcasted_iota(jnp.int32, kr.shape, 1)
    partner = jnp.where((lane & 1) == 0, -pltpu.roll(kr, LANES - 1, axis=1), pltpu.roll(kr, 1, axis=1))
    kr_ref[0] = (kr * cos_ref[0] + partner * sin_ref[0]).astype(kr_ref.dtype)


def _small_proj(h_all, w_in_t, bias, cos_k, sin_k, tm):
    B = h_all.shape[0]
    blk = lambda c: pl.BlockSpec((1, tm, c), lambda b, i: (b, i, 0))
    tab = pl.BlockSpec((1, tm, LANES), lambda b, i: (0, i, 0))
    w_rows = lambda r0: pl.BlockSpec((pl.Element(LANES), pl.Element(D_MODEL)), lambda b, i: (r0, 0))
    return pl.pallas_call(
        _small_kernel,
        grid=(B, ROWS_ALL // tm),
        in_specs=[blk(D_MODEL), w_rows(OFF_G), w_rows(OFF_KR),
                  pl.BlockSpec((1, LANES), lambda b, i: (0, 0)),
                  tab, tab],
        out_specs=[blk(LANES), blk(LANES)],
        out_shape=[jax.ShapeDtypeStruct((B, ROWS_ALL, LANES), F32),
                   jax.ShapeDtypeStruct((B, ROWS_ALL, LANES), BF16)],
        compiler_params=_params(("arbitrary", "arbitrary")),
        name="small_proj",
    )(h_all, w_in_t, w_in_t, bias, cos_k, sin_k)


def _log_sigmoid(x):
    return jnp.minimum(x, 0.0) - jnp.log1p(jnp.exp(-jnp.abs(x)))


MLSTM_HP = 8


def _mlstm_dir(q_ref, k_ref, v_ref, zg_ref, o_ref, c_ref, n_ref, m_ref, *, reverse, head0):
    L = M_CHUNK
    zg = zg_ref[0]
    lane = lax.broadcasted_iota(jnp.int32, (L, LANES), 1)
    sub = lax.broadcasted_iota(jnp.int32, (LANES, L), 0)
    row_id = lax.broadcasted_iota(jnp.int32, (L, L), 0)
    col_id = lax.broadcasted_iota(jnp.int32, (L, L), 1)
    seen = (col_id >= row_id) if reverse else (col_id <= row_id)

    lf = _log_sigmoid(zg)
    tri = jnp.where(seen, 1.0, 0.0).astype(BF16)
    hi, mid, lo = _split3(lf)
    bcum = (jnp.dot(tri, hi, preferred_element_type=F32)
            + jnp.dot(tri, mid, preferred_element_type=F32)
            + jnp.dot(tri, lo, preferred_element_type=F32))

    bcum_t = bcum.T
    zg_t = zg.T

    def pick_col(a, c):
        return jnp.sum(jnp.where(lane == c, a, 0.0), axis=1, keepdims=True)

    def pick_row(a_t, c):
        return jnp.sum(jnp.where(sub == c, a_t, 0.0), axis=0, keepdims=True)

    for hh in range(MLSTM_HP):
        col_i = head0 + hh + (2 * M_HEADS if reverse else 0)
        col_f = col_i + M_HEADS
        q = q_ref[0, :, hh * M_DQK:(hh + 1) * M_DQK]
        k = k_ref[0, :, hh * M_DQK:(hh + 1) * M_DQK]
        v = v_ref[0, :, hh * M_DV:(hh + 1) * M_DV]
        c_prev = c_ref[hh]
        n_prev = n_ref[hh]
        m_prev = m_ref[hh]

        b_col = pick_col(bcum, col_f)
        i_col = pick_col(zg, col_i)
        b_row = pick_row(bcum_t, col_f)
        i_row = pick_row(zg_t, col_i)
        b_last = b_col[0:1, :] if reverse else b_col[L - 1:L, :]

        d = jnp.where(seen, b_col - b_row + i_row, -jnp.inf)
        inter = b_col + m_prev
        m_t = jnp.maximum(inter, jnp.max(d, axis=1, keepdims=True))
        w = jnp.exp(d - m_t) * (M_DQK ** -0.5)
        scale = jnp.exp(inter - m_t)
        s = lax.dot_general(q, k, (((1,), (1,)), ((), ())), preferred_element_type=F32) * w
        cq = lax.dot_general(q, c_prev.astype(BF16), (((1,), (1,)), ((), ())),
                             preferred_element_type=F32)
        num = jnp.dot(s.astype(BF16), v, preferred_element_type=F32) + scale * cq
        nq = jnp.sum(q.astype(F32) * n_prev, axis=1, keepdims=True)
        den = jnp.sum(s, axis=1, keepdims=True) + scale * nq
        h_out = num / jnp.maximum(jnp.abs(den), jnp.exp(-m_t))
        o_ref[0, :, hh * M_DV:(hh + 1) * M_DV] = h_out.astype(o_ref.dtype)

        g = b_last - b_col + i_col
        m_new = jnp.maximum(b_last + m_prev, jnp.max(g, axis=0, keepdims=True))
        wk = jnp.exp(g - m_new) * (M_DQK ** -0.5)
        dec = jnp.exp(b_last + m_prev - m_new)
        vw = (v.astype(F32) * wk).astype(BF16)
        upd = lax.dot_general(vw, k, (((0,), (0,)), ((), ())), preferred_element_type=F32)
        c_ref[hh] = dec * c_prev + upd
        n_ref[hh] = dec * n_prev + jnp.sum(k.astype(F32) * wk, axis=0, keepdims=True)
        m_ref[hh] = m_new


def _mlstm_kernel(qf, kf, vf, gf, qb, kb, vb, gb, of, ob, cf, nf, mf, cb, nb, mb):
    head0 = pl.program_id(1) * MLSTM_HP

    @pl.when(pl.program_id(2) == 0)
    def _():
        for r in (cf, nf, mf, cb, nb, mb):
            r[...] = jnp.zeros_like(r)

    _mlstm_dir(qf, kf, vf, gf, of, cf, nf, mf, reverse=False, head0=head0)
    _mlstm_dir(qb, kb, vb, gb, ob, cb, nb, mb, reverse=True, head0=head0)


def _mlstm(z_a, zg):
    B = z_a.shape[0]
    L = M_CHUNK
    HP = MLSTM_HP
    nlat = SEQ // L
    assert CTX_LEN == L and M_HEADS % HP == 0
    k_blk0 = M_QK_W // (HP * M_DQK)
    v_blk0 = (2 * M_QK_W) // (HP * M_DV)

    def cf(j):
        return jnp.where(j == 0, nlat, j - 1)

    def cb(j):
        return jnp.where(j == 0, nlat, nlat - j)

    def specs(ch):
        return [pl.BlockSpec((1, L, HP * M_DQK), lambda b, h, j: (b, ch(j), h)),
                pl.BlockSpec((1, L, HP * M_DQK), lambda b, h, j: (b, ch(j), k_blk0 + h)),
                pl.BlockSpec((1, L, HP * M_DV), lambda b, h, j: (b, ch(j), v_blk0 + h)),
                pl.BlockSpec((1, L, LANES), lambda b, h, j: (b, ch(j), 0))]

    out_sds = jax.ShapeDtypeStruct((B, SEQ, M_V_W), BF16)
    state = [pltpu.VMEM((HP, M_DV, M_DQK), F32), pltpu.VMEM((HP, 1, M_DQK), F32),
             pltpu.VMEM((HP, 1, 1), F32)]
    return pl.pallas_call(
        _mlstm_kernel,
        grid=(B, M_HEADS // HP, nlat + 1),
        in_specs=specs(cf) + specs(cb),
        out_specs=[pl.BlockSpec((1, L, HP * M_DV), lambda b, h, j: (b, jnp.maximum(j - 1, 0), h)),
                   pl.BlockSpec((1, L, HP * M_DV), lambda b, h, j: (b, jnp.minimum(nlat - j, nlat - 1), h))],
        out_shape=[out_sds, out_sds],
        scratch_shapes=state + state,
        compiler_params=_params(("arbitrary", "arbitrary", "arbitrary")),
        name="mlstm_scan",
    )(z_a, z_a, z_a, zg, z_a, z_a, z_a, zg)


def _mlstm_out_kernel(hf_ref, hb_ref, zo_ref, g_ref, o_ref):
    for h in range(M_HEADS):
        sl = slice(h * M_DV, (h + 1) * M_DV)
        hs = hf_ref[0, :, sl].astype(F32) + hb_ref[0, :, sl].astype(F32)
        ms = jnp.mean(hs * hs, axis=-1, keepdims=True)
        hn = hs * lax.rsqrt(ms + NORM_EPS) * g_ref[:, sl]
        o_ref[0, :, sl] = (hn * jax.nn.sigmoid(zo_ref[0, :, sl].astype(F32))).astype(o_ref.dtype)


def _mlstm_out(h_f, h_b, z_o, gain, tm=256):
    B = h_f.shape[0]
    blk = pl.BlockSpec((1, tm, M_V_W), lambda b, i: (b, i, 0))
    return pl.pallas_call(
        _mlstm_out_kernel,
        grid=(B, SEQ // tm),
        in_specs=[blk, blk, blk, pl.BlockSpec((1, M_V_W), lambda b, i: (0, 0))],
        out_specs=blk,
        out_shape=jax.ShapeDtypeStruct((B, SEQ, M_V_W), BF16),
        compiler_params=_params(("arbitrary", "arbitrary")),
        name="mlstm_out",
    )(h_f, h_b, z_o, gain.reshape(1, M_V_W))


ATTN_KEY_BLOCK = 768


ATTN_HP = 2


def _attn_kernel(q_ref, kv_ref, kr_ref, o_ref):
    kvw = A_NOPE + A_DV
    m = [None] * ATTN_HP
    l = [None] * ATTN_HP
    acc = [None] * ATTN_HP
    for j in range(ROWS_ALL // ATTN_KEY_BLOCK):
        ks = slice(j * ATTN_KEY_BLOCK, (j + 1) * ATTN_KEY_BLOCK)
        kr = kr_ref[0, ks, :]
        for hh in range(ATTN_HP):
            q = q_ref[0, :, hh * A_QW:(hh + 1) * A_QW]
            kcat = jnp.concatenate([kv_ref[0, ks, hh * kvw:hh * kvw + A_NOPE], kr], axis=1)
            s = lax.dot_general(q, kcat, (((1,), (1,)), ((), ())), preferred_element_type=F32)
            m_j = jnp.max(s, axis=1, keepdims=True)
            m_new = m_j if j == 0 else jnp.maximum(m[hh], m_j)
            p = jnp.exp(s - m_new)
            l_j = jnp.sum(p, axis=1, keepdims=True)
            pv = jnp.dot(p.astype(BF16), kv_ref[0, ks, hh * kvw + A_NOPE:(hh + 1) * kvw],
                         preferred_element_type=F32)
            if j == 0:
                l[hh], acc[hh] = l_j, pv
            else:
                alpha = jnp.exp(m[hh] - m_new)
                l[hh] = alpha * l[hh] + l_j
                acc[hh] = alpha * acc[hh] + pv
            m[hh] = m_new
    for hh in range(ATTN_HP):
        o_ref[0, :, hh * A_DV:(hh + 1) * A_DV] = (acc[hh] / l[hh]).astype(o_ref.dtype)


def _attention(q, kv, krz, tq=1024):
    B = q.shape[0]
    HP = ATTN_HP
    return pl.pallas_call(
        _attn_kernel,
        grid=(B, A_HEADS // HP, SEQ // tq),
        in_specs=[pl.BlockSpec((1, tq, HP * A_QW), lambda b, h, i: (b, i, h)),
                  pl.BlockSpec((1, ROWS_ALL, HP * (A_NOPE + A_DV)), lambda b, h, i: (b, 0, h)),
                  pl.BlockSpec((1, ROWS_ALL, LANES), lambda b, h, i: (b, 0, 0))],
        out_specs=pl.BlockSpec((1, tq, HP * A_DV), lambda b, h, i: (b, i, h)),
        out_shape=jax.ShapeDtypeStruct((B, SEQ, A_HEADS * A_DV), BF16),
        compiler_params=_params(("arbitrary", "arbitrary", "arbitrary")),
        name="latent_attn",
    )(q, kv, krz)


def _post1_kernel(x_ref, y_ref, gpost_ref, gpre_ref, g1_ref, sc_ref, sh_ref, rwh_ref, rwl_ref, rb_ref,
                  x1_ref, h2_ref, eidx_ref, gates_ref, rank_ref, cnt_ref, carry_ref):
    first = jnp.logical_and(pl.program_id(0) == 0, pl.program_id(1) == 0)

    @pl.when(first)
    def _():
        carry_ref[...] = jnp.zeros_like(carry_ref)

    y = y_ref[0].astype(F32)
    yn = y * lax.rsqrt(jnp.mean(y * y, axis=-1, keepdims=True) + NORM_EPS) * gpost_ref[...]
    x1 = x_ref[0] + g1_ref[0, 0] * yn
    x1_ref[0] = x1
    hn = x1 * lax.rsqrt(jnp.mean(x1 * x1, axis=-1, keepdims=True) + NORM_EPS) * gpre_ref[...]
    h2 = hn * (1.0 + sc_ref[0, 0]) + sh_ref[0, 0]
    h2_ref[0] = h2

    h_hi = h2.astype(BF16)
    h_lo = (h2 - h_hi.astype(F32)).astype(BF16)
    logits = (jnp.dot(h_hi, rwh_ref[...], preferred_element_type=F32)
              + jnp.dot(h_lo, rwh_ref[...], preferred_element_type=F32)
              + jnp.dot(h_hi, rwl_ref[...], preferred_element_type=F32)) + rb_ref[...]
    tm = logits.shape[0]
    lane = lax.broadcasted_iota(jnp.int32, (tm, LANES), 1)
    lane_f = lane.astype(F32)
    work = jnp.where(lane < N_EXPERTS, logits, -jnp.inf)
    sel = jnp.zeros((tm, LANES), F32)
    tops, hots = [], []
    for _ in range(TOP_K):
        mx = jnp.max(work, axis=1, keepdims=True)
        idx = jnp.min(jnp.where(work == mx, lane_f, float(LANES)), axis=1, keepdims=True)
        hot = lane_f == idx
        tops.append(mx)
        hots.append(hot)
        sel = jnp.where(hot, 1.0, sel)
        work = jnp.where(hot, -jnp.inf, work)
    es = [jnp.exp(t - tops[0]) for t in tops]
    tot = es[0] + es[1] + es[2] + es[3]

    r_id = lax.broadcasted_iota(jnp.int32, (tm, tm), 0)
    c_id = lax.broadcasted_iota(jnp.int32, (tm, tm), 1)
    strict = jnp.where(c_id < r_id, 1.0, 0.0).astype(BF16)
    rank_all = jnp.dot(strict, sel.astype(BF16), preferred_element_type=F32) + carry_ref[...]
    carry_ref[...] = carry_ref[...] + jnp.sum(sel, axis=0, keepdims=True)
    cnt_ref[...] = carry_ref[...]

    eidx = jnp.zeros((tm, LANES), jnp.int32)
    gates = jnp.zeros((tm, LANES), F32)
    rank = jnp.zeros((tm, LANES), jnp.int32)
    for kk in range(TOP_K):
        e_k = jnp.sum(jnp.where(hots[kk], lane_f, 0.0), axis=1, keepdims=True)
        r_k = jnp.sum(jnp.where(hots[kk], rank_all, 0.0), axis=1, keepdims=True)
        eidx = jnp.where(lane == kk, e_k.astype(jnp.int32), eidx)
        gates = jnp.where(lane == kk, es[kk] / tot, gates)
        rank = jnp.where(lane == kk, r_k.astype(jnp.int32), rank)
    eidx_ref[0] = eidx
    gates_ref[0] = gates
    rank_ref[0] = rank


def _post1(x, y, gpost, gpre, mod6, rw_hi, rw_lo, rb, tm=256):
    B = x.shape[0]
    blk = pl.BlockSpec((1, tm, D_MODEL), lambda b, i: (b, i, 0))
    vec = pl.BlockSpec((1, D_MODEL), lambda b, i: (0, 0))
    lan = pl.BlockSpec((1, tm, LANES), lambda b, i: (b, i, 0))
    rw = pl.BlockSpec((D_MODEL, LANES), lambda b, i: (0, 0))

    def mod_spec(k):
        return pl.BlockSpec((1, 1, 1, D_MODEL), lambda b, i: (k, b, 0, 0))

    lan_i = jax.ShapeDtypeStruct((B, SEQ, LANES), jnp.int32)
    return pl.pallas_call(
        _post1_kernel,
        grid=(B, SEQ // tm),
        in_specs=[blk, blk, vec, vec, mod_spec(2), mod_spec(4), mod_spec(3), rw, rw,
                  pl.BlockSpec((1, LANES), lambda b, i: (0, 0))],
        out_specs=[blk, blk, lan, lan, lan,
                   pl.BlockSpec((1, LANES), lambda b, i: (0, 0))],
        out_shape=[jax.ShapeDtypeStruct((B, SEQ, D_MODEL), F32),
                   jax.ShapeDtypeStruct((B, SEQ, D_MODEL), F32),
                   lan_i, jax.ShapeDtypeStruct((B, SEQ, LANES), F32), lan_i,
                   jax.ShapeDtypeStruct((1, LANES), F32)],
        scratch_shapes=[pltpu.VMEM((1, LANES), F32)],
        compiler_params=_params(("arbitrary", "arbitrary")),
        name="post_mix",
    )(x, y, gpost.reshape(1, D_MODEL), gpre.reshape(1, D_MODEL), mod6, mod6, mod6, rw_hi, rw_lo, rb)


DMA_ISSUE_UNROLL = 8


def _gather_kernel(idx_ref, tv_ref, src_ref, o_ref, land_ref, sem):
    i = pl.program_id(0)
    n = pl.num_programs(0)
    slot = i % 2

    def for_row_multiple(rows, fn):
        for m in range(MOE_ROW_STEP, MOE_TILE + 1, MOE_ROW_STEP):
            pl.when(jnp.logical_and(rows > m - MOE_ROW_STEP, rows <= m))(functools.partial(fn, m))

    def issue(t, sl):
        base = t * MOE_TILE

        def start(j, c):
            for prio in range(2):
                r = 2 * j + prio
                pltpu.make_async_copy(src_ref.at[pl.ds(idx_ref[base + r], 1)], land_ref.at[sl, pl.ds(r, 1)],
                                      sem.at[sl]).start(priority=prio)
            return c

        def start_rows(m):
            lax.fori_loop(0, m // 2, start, 0, unroll=DMA_ISSUE_UNROLL // 2)

        for_row_multiple(tv_ref[t], start_rows)

    @pl.when(i == 0)
    def _():
        land_ref[...] = jnp.zeros_like(land_ref)
        issue(0, 0)

    @pl.when(i + 1 < n)
    def _():
        issue(jnp.minimum(i + 1, n - 1), 1 - slot)

    def wait(m):
        pltpu.make_async_copy(src_ref.at[pl.ds(0, m)], land_ref.at[slot, pl.ds(0, m)], sem.at[slot]).wait()

    for_row_multiple(tv_ref[i], wait)

    @pl.when(tv_ref[i] > 0)
    def _():
        o_ref[...] = land_ref[slot].astype(o_ref.dtype)

    @pl.when(tv_ref[i] == 0)
    def _():
        o_ref[...] = jnp.zeros_like(o_ref)


def _gather_rows(slot_tok, tile_v, src):
    n = slot_tok.shape[0]
    return pl.pallas_call(
        _gather_kernel,
        grid_spec=pltpu.PrefetchScalarGridSpec(
            num_scalar_prefetch=2,
            grid=(n // MOE_TILE,),
            in_specs=[pl.BlockSpec(memory_space=pl.ANY)],
            out_specs=pl.BlockSpec((MOE_TILE, D_MODEL), lambda i, idx, tv: (i, 0)),
            scratch_shapes=[pltpu.VMEM((2, MOE_TILE, D_MODEL), F32),
                            pltpu.SemaphoreType.DMA((2,))]),
        out_shape=jax.ShapeDtypeStruct((n, D_MODEL), BF16),
        compiler_params=_params(("arbitrary",)),
        name="moe_gather",
    )(slot_tok, tile_v, src)


MOE_ROW_STEP = 128


def _for_row_count(rows, run, o_ref):
    for m in range(MOE_ROW_STEP, MOE_TILE + 1, MOE_ROW_STEP):
        @pl.when(jnp.logical_and(rows > m - MOE_ROW_STEP, rows <= m))
        def _(m=m):
            run(m)
            if m < MOE_TILE:
                o_ref[m:, :] = jnp.zeros((MOE_TILE - m, o_ref.shape[1]), o_ref.dtype)

    @pl.when(rows == 0)
    def _():
        o_ref[...] = jnp.zeros_like(o_ref)


def _stream_expert_weights(te_ref, fr_ref, ne_ref, lg_ref, copies, cast):
    p = pl.program_id(0)
    i = pl.program_id(1)

    @pl.when(fr_ref[i] == 1)
    def _():
        @pl.when(jnp.logical_and(p == 0, i == 0))
        def _():
            for c in copies(te_ref[i], p):
                c.start()

        for c in copies(te_ref[i], p):
            c.wait()
        cast()
        is_last_group = lg_ref[i] == 1

        @pl.when(jnp.logical_not(jnp.logical_and(is_last_group, p == pl.num_programs(0) - 1)))
        def _():
            for c in copies(ne_ref[i], p + lg_ref[i]):
                c.start()


def _gmm1_kernel(te_ref, tv_ref, fr_ref, ne_ref, lg_ref, x_ref, bg_ref, bu_ref, w_hbm, o_ref,
                 land_ref, wb_ref, sem):
    i = pl.program_id(1)
    tn = o_ref.shape[1]

    def copies(e, p):
        c0 = pl.multiple_of(p * tn, tn)
        return [pltpu.make_async_copy(w_hbm.at[e, :, pl.ds(c0 + off, tn)], land_ref.at[s], sem.at[s])
                for s, off in enumerate((0, D_EXPERT))]

    def cast():
        for s in range(2):
            wb_ref[s] = land_ref[s].astype(BF16)

    _stream_expert_weights(te_ref, fr_ref, ne_ref, lg_ref, copies, cast)

    def run(m):
        xv = x_ref[:m, :]
        glu = jnp.dot(xv, wb_ref[0], preferred_element_type=F32) + bg_ref[0]
        lin = jnp.dot(xv, wb_ref[1], preferred_element_type=F32) + bu_ref[0]
        glu = jnp.minimum(glu, SWIGLU_LIMIT)
        lin = jnp.clip(lin, -SWIGLU_LIMIT, SWIGLU_LIMIT)
        o_ref[:m, :] = (glu * jax.nn.sigmoid(SWIGLU_ALPHA * glu) * (lin + 1.0)).astype(o_ref.dtype)

    _for_row_count(tv_ref[i], run, o_ref)


def _gmm1(tiles, xs, w_gu, b_gu, tn=512):
    n_slots = xs.shape[0]
    n_tiles = n_slots // MOE_TILE
    nf = D_EXPERT // tn
    b3 = b_gu.reshape(N_EXPERTS, 1, 2 * D_EXPERT)
    return pl.pallas_call(
        _gmm1_kernel,
        grid_spec=pltpu.PrefetchScalarGridSpec(
            num_scalar_prefetch=5,
            grid=(nf, n_tiles),
            in_specs=[pl.BlockSpec((MOE_TILE, D_MODEL), lambda n, i, te, *_: (i, 0)),
                      pl.BlockSpec((1, 1, tn), lambda n, i, te, *_: (te[i], 0, n)),
                      pl.BlockSpec((1, 1, tn), lambda n, i, te, *_: (te[i], 0, nf + n)),
                      pl.BlockSpec(memory_space=pl.ANY)],
            out_specs=pl.BlockSpec((MOE_TILE, tn), lambda n, i, te, *_: (i, n)),
            scratch_shapes=[pltpu.VMEM((2, D_MODEL, tn), F32), pltpu.VMEM((2, D_MODEL, tn), BF16),
                            pltpu.SemaphoreType.DMA((2,))]),
        out_shape=jax.ShapeDtypeStruct((n_slots, D_EXPERT), BF16),
        compiler_params=_params(("arbitrary", "arbitrary")),
        name="moe_gate_up",
    )(*tiles, xs, b3, b3, w_gu)


def _gmm2_kernel(te_ref, tv_ref, fr_ref, ne_ref, lg_ref, a_ref, b_ref, w_hbm, o_ref, land_ref, wb_ref, sem):
    i = pl.program_id(1)
    tn = wb_ref.shape[1]

    def copies(e, p):
        return [pltpu.make_async_copy(w_hbm.at[e, :, pl.ds(pl.multiple_of(p * tn, tn), tn)], land_ref, sem)]

    def cast():
        wb_ref[...] = land_ref[...].astype(BF16)

    _stream_expert_weights(te_ref, fr_ref, ne_ref, lg_ref, copies, cast)

    def run(m):
        o_ref[:m, :] = jnp.dot(a_ref[:m, :], wb_ref[...], preferred_element_type=F32) + b_ref[0]

    _for_row_count(tv_ref[i], run, o_ref)


def _gmm2(tiles, act, w_down, b_down, tn):
    n_slots = act.shape[0]
    n_tiles = n_slots // MOE_TILE
    nn = D_MODEL // tn
    return pl.pallas_call(
        _gmm2_kernel,
        grid_spec=pltpu.PrefetchScalarGridSpec(
            num_scalar_prefetch=5,
            grid=(nn, n_tiles),
            in_specs=[pl.BlockSpec((MOE_TILE, D_EXPERT), lambda n, i, te, *_: (i, 0)),
                      pl.BlockSpec((1, 1, tn), lambda n, i, te, *_: (te[i], 0, n)),
                      pl.BlockSpec(memory_space=pl.ANY)],
            out_specs=pl.BlockSpec((MOE_TILE, tn), lambda n, i, te, *_: (i, n)),
            scratch_shapes=[pltpu.VMEM((D_EXPERT, tn), F32), pltpu.VMEM((D_EXPERT, tn), BF16),
                            pltpu.SemaphoreType.DMA]),
        out_shape=jax.ShapeDtypeStruct((n_slots, D_MODEL), F32),
        compiler_params=_params(("arbitrary", "arbitrary")),
        name="moe_down",
    )(*tiles, act, b_down.reshape(N_EXPERTS, 1, D_MODEL), w_down)


COMBINE_TM = 128
GMM2_TN = 2048


def _combine_kernel(dest_ref, x1_ref, gates_ref, gpost_ref, g2_ref, ys_ref, o_ref, buf_ref, sem):
    tm = COMBINE_TM
    n_rows = tm * TOP_K
    n_steps = pl.num_programs(0) * pl.num_programs(1)
    step = pl.program_id(0) * pl.num_programs(1) + pl.program_id(1)
    slot = step % 2

    def issue(st, sl):
        base = st * n_rows

        def start(j, c):
            for prio in range(2):
                r = 2 * j + prio
                pltpu.make_async_copy(ys_ref.at[pl.ds(dest_ref[base + r], 1)],
                                      buf_ref.at[sl, pl.ds(r, 1)], sem.at[sl]).start(priority=prio)
            return c

        lax.fori_loop(0, n_rows // 2, start, 0, unroll=DMA_ISSUE_UNROLL // 2)

    @pl.when(step == 0)
    def _():
        issue(0, 0)

    @pl.when(step + 1 < n_steps)
    def _():
        issue(step + 1, 1 - slot)

    pltpu.make_async_copy(ys_ref.at[pl.ds(0, n_rows)], buf_ref.at[slot], sem.at[slot]).wait()

    gates = gates_ref[0]
    lane = lax.broadcasted_iota(jnp.int32, gates.shape, 1)
    f = None
    for kk in range(TOP_K):
        gk = jnp.sum(jnp.where(lane == kk, gates, 0.0), axis=1, keepdims=True)
        term = gk * buf_ref[slot, pl.ds(kk * tm, tm), :]
        f = term if f is None else f + term
    fn = f * lax.rsqrt(jnp.mean(f * f, axis=-1, keepdims=True) + NORM_EPS) * gpost_ref[...]
    o_ref[0] = x1_ref[0] + g2_ref[0, 0] * fn


def _combine(dest, x1, gates, gpost, mod6, ys):
    B = x1.shape[0]
    tm = COMBINE_TM
    blk = pl.BlockSpec((1, tm, D_MODEL), lambda b, i, d: (b, i, 0))
    return pl.pallas_call(
        _combine_kernel,
        grid_spec=pltpu.PrefetchScalarGridSpec(
            num_scalar_prefetch=1,
            grid=(B, SEQ // tm),
            in_specs=[blk,
                      pl.BlockSpec((1, tm, LANES), lambda b, i, d: (b, i, 0)),
                      pl.BlockSpec((1, D_MODEL), lambda b, i, d: (0, 0)),
                      pl.BlockSpec((1, 1, 1, D_MODEL), lambda b, i, d: (5, b, 0, 0)),
                      pl.BlockSpec(memory_space=pl.ANY)],
            out_specs=blk,
            scratch_shapes=[pltpu.VMEM((2, tm * TOP_K, D_MODEL), F32),
                            pltpu.SemaphoreType.DMA((2,))]),
        out_shape=jax.ShapeDtypeStruct((B, SEQ, D_MODEL), F32),
        compiler_params=_params(("arbitrary", "arbitrary")),
        name="moe_combine",
    )(dest, x1, gates, gpost.reshape(1, D_MODEL), mod6, ys)


def _rope_tables():
    rows = SEQ // GRID_W
    r, col = jnp.meshgrid(jnp.arange(rows, dtype=F32), jnp.arange(GRID_W, dtype=F32), indexing="ij")
    n_freq = A_ROPE // 4
    inv = ROPE_BASE ** (-jnp.arange(n_freq, dtype=F32) / n_freq)
    ang = jnp.concatenate([r.reshape(-1, 1) * inv, col.reshape(-1, 1) * inv], axis=-1)
    cos = jnp.concatenate([jnp.repeat(jnp.cos(ang), 2, axis=-1), jnp.ones((CTX_LEN, A_ROPE), F32)], axis=0)
    sin = jnp.concatenate([jnp.repeat(jnp.sin(ang), 2, axis=-1), jnp.zeros((CTX_LEN, A_ROPE), F32)], axis=0)
    pad = ((0, 0), (0, LANES - A_ROPE))
    return jnp.concatenate([cos, sin], axis=-1)[None], jnp.pad(cos, pad)[None], jnp.pad(sin, pad)[None]


def _rot_partner(w):
    wp = w.reshape(*w.shape[:-1], A_ROPE // 2, 2)
    return jnp.stack([-wp[..., 1], wp[..., 0]], axis=-1).reshape(w.shape)


def _layer(x, ctx, mod6, tabs, norm_pre_mix, norm_post_mix, norm_pre_ffn, norm_post_ffn, w_in, b_gates,
           m_out_norm, q_norm, kv_norm, w_uq, w_ukv, w_branch_a, w_branch_b, w_out, router_w, router_b,
           w_gu, b_gu, w_down, b_down):
    B = x.shape[0]

    tab, cos_k, sin_k = tabs
    w_in_t = w_in.T
    gate_bias = jnp.concatenate([b_gates, jnp.zeros((LANES - N_GATE_COLS,), F32)]).reshape(1, LANES)
    wq = w_uq.reshape(Q_LORA, A_HEADS, A_NOPE + A_ROPE)
    wq_r = wq[..., A_NOPE:]
    w_uq2 = jnp.concatenate([wq[..., :A_NOPE], wq_r, _rot_partner(wq_r)], axis=-1)
    w_uq2 = w_uq2.reshape(Q_LORA, A_HEADS * A_QW).astype(BF16)

    h_all = _prenorm(x, ctx, norm_pre_mix, mod6)

    off_o = 2 * M_QK_W + M_V_W
    z_a = _matmul(h_all, w_in_t, rows=ROWS_ALL, tm=1152, tn=512, n_out=off_o, out_dtype=BF16, wt_row0=OFF_Q,
                  name="in_proj_a")
    z_o = _matmul(h_all, w_in_t, rows=SEQ, tm=1024, tn=512, n_out=M_V_W, out_dtype=BF16, wt_row0=off_o,
                  name="in_proj_o")
    z_b = _matmul(h_all, w_in_t, rows=ROWS_ALL, tm=1152, tn=512, n_out=Q_LORA + KV_LORA, out_dtype=BF16,
                  wt_row0=OFF_CQ, name="in_proj_b")
    z_d = _matmul(h_all, w_in_t, rows=SEQ, tm=1024, tn=512, n_out=2 * D_MODEL, out_dtype=BF16, wt_row0=OFF_GA,
                  name="in_proj_d")
    zg, krz = _small_proj(h_all, w_in_t, gate_bias, cos_k, sin_k, tm=1152)

    h_f, h_b = _mlstm(z_a, zg)
    ya = _mlstm_out(h_f, h_b, z_o, m_out_norm)

    q = _matmul(z_b, w_uq2, rows=SEQ, tm=1024, tn=1024, n_out=A_HEADS * A_QW, out_dtype=BF16, k=Q_LORA,
                rms_gain=q_norm, rope_tab=tab, scale=A_SCALE, name="q_up")
    kv = _matmul(z_b, w_ukv, rows=ROWS_ALL, tm=1152, tn=1024, n_out=A_HEADS * (A_NOPE + A_DV), out_dtype=BF16,
                 k=KV_LORA, x_col=Q_LORA // KV_LORA, rms_gain=kv_norm, name="kv_up")
    yb = _attention(q, kv, krz)

    ua = _matmul(ya, w_branch_a, rows=SEQ, tm=1024, tn=512, n_out=D_MODEL, out_dtype=BF16, gate=z_d,
                 name="branch_a")
    u = _matmul(yb, w_branch_b, rows=SEQ, tm=1024, tn=512, n_out=D_MODEL, out_dtype=BF16, gate=z_d,
                gate_col=D_MODEL // 512, add=ua, name="branch_b")
    y = _matmul(u, w_out, rows=SEQ, tm=1024, tn=512, n_out=D_MODEL, out_dtype=BF16, name="out_proj")

    rw = jnp.concatenate([router_w, jnp.zeros((D_MODEL, LANES - N_EXPERTS), F32)], axis=1)
    rw_hi = rw.astype(BF16)
    rw_lo = (rw - rw_hi.astype(F32)).astype(BF16)
    rb = jnp.concatenate([router_b, jnp.zeros((LANES - N_EXPERTS,), F32)]).reshape(1, LANES)
    x1, h2p, eidx, gates, rank, counts = _post1(x, y, norm_post_mix, norm_pre_ffn, mod6, rw_hi, rw_lo, rb)

    T = B * SEQ
    n_tiles = -(-(T * TOP_K + N_EXPERTS * (MOE_TILE - 1)) // MOE_TILE)
    n_slots = n_tiles * MOE_TILE
    cnt = counts[0, :N_EXPERTS].astype(jnp.int32)
    padded = (cnt + MOE_TILE - 1) // MOE_TILE * MOE_TILE
    g_end = jnp.cumsum(padded)
    g_start = g_end - padded
    e_flat = eidx.reshape(T, LANES)[:, :TOP_K]
    dest = g_start[e_flat] + rank.reshape(T, LANES)[:, :TOP_K]
    tok = jnp.broadcast_to(jnp.arange(T, dtype=jnp.int32)[:, None], (T, TOP_K))
    slot_tok = jnp.zeros((n_slots,), jnp.int32).at[dest.reshape(-1)].set(tok.reshape(-1))
    tile_lo = jnp.arange(n_tiles, dtype=jnp.int32) * MOE_TILE
    tile_e = jnp.minimum(jnp.sum((g_end[None, :] <= tile_lo[:, None]).astype(jnp.int32), axis=1), N_EXPERTS - 1)
    tile_v = jnp.clip(g_start[tile_e] + cnt[tile_e] - tile_lo, 0, MOE_TILE) * (tile_lo < g_end[-1])
    tile_v = tile_v.astype(jnp.int32)
    prev_e = jnp.concatenate([jnp.full((1,), -1, jnp.int32), tile_e[:-1]])
    tile_first = jnp.logical_and(tile_v > 0, tile_e != prev_e).astype(jnp.int32)
    after = g_end[tile_e] // MOE_TILE
    has_after = after * MOE_TILE < g_end[-1]
    tile_next_e = jnp.where(has_after, tile_e[jnp.minimum(after, n_tiles - 1)], tile_e[0]).astype(jnp.int32)
    tile_last = jnp.logical_not(has_after).astype(jnp.int32)
    tiles = (tile_e, tile_v, tile_first, tile_next_e, tile_last)
    dest_t = dest.reshape(T // COMBINE_TM, COMBINE_TM, TOP_K).transpose(0, 2, 1).reshape(-1)

    xs = _gather_rows(slot_tok, tile_v, h2p.reshape(T, D_MODEL))
    act = _gmm1(tiles, xs, w_gu, b_gu)
    ys = _gmm2(tiles, act, w_down, b_down, tn=GMM2_TN)
    return _combine(dest_t, x1, gates, norm_post_ffn, mod6, ys)


def kernel(x, c, ctx, c_ctx, w_ada, b_ada, norm_pre_mix, norm_post_mix, norm_pre_ffn, norm_post_ffn, w_in,
           b_gates, m_out_norm, q_norm, kv_norm, w_uq, w_ukv, w_branch_a, w_branch_b, w_out, router_w, router_b,
           w_gu, b_gu, w_down, b_down):
    B = x.shape[0]
    depth = w_ada.shape[0]
    assert depth == 1, "context-stream outputs between layers are not implemented"
    tabs = _rope_tables()
    cvec = jnp.concatenate([c, c_ctx[None], jnp.zeros((8 - B - 1, D_MODEL), F32)], axis=0)
    l = 0
    mod = _ada(cvec, w_ada[l], b_ada[l])
    mod6 = mod.reshape(8, N_MOD, 1, D_MODEL).transpose(1, 0, 2, 3)
    return _layer(x, ctx, mod6, tabs, norm_pre_mix[l], norm_post_mix[l], norm_pre_ffn[l], norm_post_ffn[l],
                  w_in[l], b_gates[l], m_out_norm[l], q_norm[l], kv_norm[l], w_uq[l], w_ukv[l], w_branch_a[l],
                  w_branch_b[l], w_out[l], router_w[l], router_b[l], w_gu[l], b_gu[l], w_down[l], b_down[l])
```
